```python
import math
import jax
import jax.numpy as jnp
from jax import lax
import numpy as np

D_MODEL = 1024
BATCH = 8
SEQ = 8192
DEPTH = 4
DEC_BATCH = 16
DEC_SEQ = 16
PAST_LEN = 2048

CHUNK = 64
Q_BLOCK = 128
ROPE_THETA = 10000.0
EPS = 1e-6
MLA_HEADS = 8
MLA_Q_LORA = 256
MLA_KV_LORA = 128
MLA_NOPE = 64
MLA_ROPE = 32
MLA_V = 64
DIFF_HEADS = 4
DIFF_HD = 64
DIFF_QKV = DIFF_HEADS * 2 * DIFF_HD
MIX_WIDTH = MLA_HEADS * MLA_V + DIFF_QKV
IN_SIZES = (MLA_Q_LORA, MLA_KV_LORA, MLA_ROPE, DIFF_QKV, DIFF_QKV, DIFF_QKV)
IN_COLS = sum(IN_SIZES)
N_EXPERTS = 32
TOP_K = 4
D_EXPERT = 1024
SWIGLU_LIMIT = 7.0
SWIGLU_ALPHA = 1.702
MOE_BLOCK = 128
ADA_SCALE = 0.5

kernel_name = "hybrid_mla_diffattn_moe_streaming_step"


def rmsnorm(x, g):
    xf = x.astype(jnp.float32)
    y = xf * lax.rsqrt(jnp.mean(xf * xf, axis=-1, keepdims=True) + EPS)
    return (y * g.astype(jnp.float32)).astype(x.dtype)


def rope(x, pos):
    d = x.shape[-1]
    inv = ROPE_THETA ** (-jnp.arange(0, d, 2, dtype=jnp.float32) / d)
    ang = pos.astype(jnp.float32)[:, None] * inv[None, :]
    ang = ang.reshape((ang.shape[0],) + (1,) * (x.ndim - 3) + (d // 2,))
    cos, sin = jnp.cos(ang), jnp.sin(ang)
    xf = x.astype(jnp.float32)
    x1, x2 = xf[..., : d // 2], xf[..., d // 2:]
    return jnp.concatenate([x1 * cos - x2 * sin, x1 * sin + x2 * cos], axis=-1).astype(x.dtype)


def chunk_attention(q, k, v, q_pos, k_pos, scale):
    k_chunk = k_pos // CHUNK

    def block(args):
        qb, qp = args
        s = jnp.einsum('bqhd,bkhd->bhqk', qb, k).astype(jnp.float32) * scale
        mask = k_chunk[None, :] <= (qp // CHUNK)[:, None]
        s = jnp.where(mask[None, None], s, -jnp.inf)
        p = jax.nn.softmax(s, axis=-1)
        return jnp.einsum('bhqk,bkhd->bqhd', p.astype(v.dtype), v)

    b, sq, h, dq = q.shape
    if sq <= Q_BLOCK:
        return block((q, q_pos))
    nb = sq // Q_BLOCK
    qs = q.reshape(b, nb, Q_BLOCK, h, dq).transpose(1, 0, 2, 3, 4)
    out = lax.map(block, (qs, q_pos.reshape(nb, Q_BLOCK)))
    return out.transpose(1, 0, 2, 3, 4).reshape(b, sq, h, v.shape[-1])


def mixer(h, pos, past, lam_init, p):
    b, s, _ = h.shape
    proj = h @ p['w_in']
    cuts = np.cumsum(IN_SIZES)[:-1].tolist()
    cq, ckv, kr, dq, dk, dv = jnp.split(proj, cuts, axis=-1)
    q = (rmsnorm(cq, p['g_q']) @ p['w_uq']).reshape(b, s, MLA_HEADS, MLA_NOPE + MLA_ROPE)
    q = jnp.concatenate([q[..., :MLA_NOPE], rope(q[..., MLA_NOPE:], pos)], axis=-1)
    latent = rmsnorm(ckv, p['g_kv'])
    k_rope = rope(kr[:, :, None, :], pos)[:, :, 0, :]
    dq = rope(dq.reshape(b, s, DIFF_HEADS, 2, DIFF_HD), pos)
    dk = rope(dk.reshape(b, s, DIFF_HEADS, 2, DIFF_HD), pos).reshape(b, s, DIFF_HEADS, 2 * DIFF_HD)
    dv = dv.reshape(b, s, DIFF_HEADS, 2 * DIFF_HD)
    if past is None:
        lat_all, kr_all, dk_all, dv_all, k_pos = latent, k_rope, dk, dv, pos
    else:
        p_lat, p_kr, p_dk, p_dv = past
        lat_all = jnp.concatenate([p_lat, latent], axis=1)
        kr_all = jnp.concatenate([p_kr, k_rope], axis=1)
        dk_all = jnp.concatenate([p_dk, dk], axis=1)
        dv_all = jnp.concatenate([p_dv, dv], axis=1)
        k_pos = jnp.concatenate([jnp.arange(p_lat.shape[1], dtype=jnp.int32), pos])
    sk = lat_all.shape[1]
    k_nope = jnp.einsum('bsc,chd->bshd', lat_all, p['w_uk'])
    v_mla = jnp.einsum('bsc,chd->bshd', lat_all, p['w_uv'])
    k_mla = jnp.concatenate(
        [k_nope, jnp.broadcast_to(kr_all[:, :, None, :], (b, sk, MLA_HEADS, MLA_ROPE))], axis=-1)
    o_mla = chunk_attention(q, k_mla, v_mla, pos, k_pos, 1.0 / math.sqrt(MLA_NOPE + MLA_ROPE))
    dk2 = dk_all.reshape(b, sk, DIFF_HEADS, 2, DIFF_HD)
    sc = 1.0 / math.sqrt(DIFF_HD)
    o1 = chunk_attention(dq[..., 0, :], dk2[..., 0, :], dv_all, pos, k_pos, sc)
    o2 = chunk_attention(dq[..., 1, :], dk2[..., 1, :], dv_all, pos, k_pos, sc)
    f32 = jnp.float32
    lam = (jnp.exp(jnp.sum(p['lam_q1'].astype(f32) * p['lam_k1'].astype(f32)))
           - jnp.exp(jnp.sum(p['lam_q2'].astype(f32) * p['lam_k2'].astype(f32))) + lam_init)
    o_diff = (o1.astype(f32) - lam * o2.astype(f32)).astype(h.dtype)
    o_diff = rmsnorm(o_diff, p['g_diff']) * (1.0 - lam_init)
    merged = jnp.concatenate([o_mla.reshape(b, s, MLA_HEADS * MLA_V),
                              o_diff.reshape(b, s, DIFF_QKV).astype(h.dtype)], axis=-1)
    return merged @ p['w_out'], (latent, k_rope, dk, dv)


def moe(x, w_router, b_router, w_gu, b_gu, w_down, b_down):
    n, d = x.shape
    logits = x.astype(jnp.float32) @ w_router.astype(jnp.float32) + b_router.astype(jnp.float32)
    top_val, top_idx = lax.top_k(logits, TOP_K)
    gates = jax.nn.softmax(top_val, axis=-1).astype(x.dtype)
    nk = n * TOP_K
    flat_e = top_idx.reshape(nk)
    flat_tok = jnp.repeat(jnp.arange(n, dtype=jnp.int32), TOP_K)
    order = jnp.argsort(flat_e, stable=True)
    sorted_e = flat_e[order]
    counts = jnp.bincount(flat_e, length=N_EXPERTS)
    starts = jnp.cumsum(counts) - counts
    padded = (counts + MOE_BLOCK - 1) // MOE_BLOCK * MOE_BLOCK
    pad_ends = jnp.cumsum(padded)
    pad_starts = pad_ends - padded
    dest = pad_starts[sorted_e] + jnp.arange(nk, dtype=jnp.int32) - starts[sorted_e]
    n_blocks = -(-nk // MOE_BLOCK) + N_EXPERTS
    slots = n_blocks * MOE_BLOCK
    slot_tok = jnp.full((slots,), n, jnp.int32).at[dest].set(flat_tok[order])
    slot_gate = jnp.zeros((slots,), x.dtype).at[dest].set(gates.reshape(nk)[order])
    block_e = jnp.minimum(
        jnp.searchsorted(pad_ends, jnp.arange(n_blocks, dtype=jnp.int32) * MOE_BLOCK, side='right'),
        N_EXPERTS - 1)
    x_pad = jnp.concatenate([x, jnp.zeros((1, d), x.dtype)], axis=0)

    def expert_block(args):
        tok, e = args
        gu = x_pad[tok] @ w_gu[e] + b_gu[e]
        glu, lin = jnp.split(gu, 2, axis=-1)
        glu = jnp.minimum(glu, SWIGLU_LIMIT)
        lin = jnp.clip(lin, -SWIGLU_LIMIT, SWIGLU_LIMIT)
        act = glu * jax.nn.sigmoid(SWIGLU_ALPHA * glu) * (lin + 1.0)
        return act @ w_down[e] + b_down[e]

    out = lax.map(expert_block, (slot_tok.reshape(n_blocks, MOE_BLOCK), block_e))
    y = jnp.zeros((n + 1, d), x.dtype).at[slot_tok].add(out.reshape(slots, d) * slot_gate[:, None])
    return y[:n]


LAYER_KEYS = ('w_in', 'g_q', 'w_uq', 'g_kv', 'w_uk', 'w_uv', 'lam_q1', 'lam_k1',
              'lam_q2', 'lam_k2', 'g_diff', 'w_out')


def trunk(x, c, pos, caches, prm):
    b, s, d = x.shape
    new_rows = ([], [], [], [])
    for l in range(DEPTH):
        mod = jax.nn.silu(c) @ prm['w_ada'][l] + prm['b_ada'][l]
        sh1, sc1, g1, sh2, sc2, g2 = jnp.split(mod[:, None, :], 6, axis=-1)
        h = rmsnorm(x, prm['g_attn'][l]) * (1.0 + sc1) + sh1
        past = None if caches is None else tuple(cc[l] for cc in caches)
        lp = {k: prm[k][l] for k in LAYER_KEYS}
        lam_init = 0.8 - 0.6 * math.exp(-0.3 * l)
        a, rows = mixer(h, pos, past, lam_init, lp)
        x = x + g1 * a
        h = rmsnorm(x, prm['g_ffn'][l]) * (1.0 + sc2) + sh2
        f = moe(h.reshape(b * s, d), prm['w_router'][l], prm['b_router'][l], prm['w_gate_up'][l],
                prm['b_gate_up'][l], prm['w_down'][l], prm['b_down'][l]).reshape(b, s, d)
        x = x + g2 * f
        for lst, r in zip(new_rows, rows):
            lst.append(r)
    y = rmsnorm(x, prm['g_final'])
    return y, [jnp.stack(lst) for lst in new_rows]


def setup_inputs(seed: int = 0) -> dict:
    key = jax.random.key(seed)
    ks = iter(jax.random.split(key, 40))

    def nrm(shape, scale):
        return jax.random.normal(next(ks), shape, jnp.float32) * scale

    D, L, E, F = D_MODEL, DEPTH, N_EXPERTS, D_EXPERT
    return {
        'x_prompt': nrm((BATCH, SEQ, D), 1.0),
        'x_sample': nrm((DEC_BATCH, DEC_SEQ, D), 1.0),
        'c_prompt': nrm((BATCH, D), 1.0),
        'c_sample': nrm((DEC_BATCH, D), 1.0),
        'cache_mla_latent': nrm((L, DEC_BATCH, PAST_LEN, MLA_KV_LORA), 1.0),
        'cache_mla_krope': nrm((L, DEC_BATCH, PAST_LEN, MLA_ROPE), 1.0),
        'cache_diff_k': nrm((L, DEC_BATCH, PAST_LEN, DIFF_HEADS, 2 * DIFF_HD), 1.0),
        'cache_diff_v': nrm((L, DEC_BATCH, PAST_LEN, DIFF_HEADS, 2 * DIFF_HD), 1.0),
        'w_ada': nrm((L, D, 6 * D), ADA_SCALE * D ** -0.5),
        'b_ada': nrm((L, 6 * D), 0.02),
        'g_attn': 1.0 + nrm((L, D), 0.05),
        'g_ffn': 1.0 + nrm((L, D), 0.05),
        'w_in': nrm((L, D, IN_COLS), D ** -0.5),
        'g_q': 1.0 + nrm((L, MLA_Q_LORA), 0.05),
        'w_uq': nrm((L, MLA_Q_LORA, MLA_HEADS * (MLA_NOPE + MLA_ROPE)), MLA_Q_LORA ** -0.5),
        'g_kv': 1.0 + nrm((L, MLA_KV_LORA), 0.05),
        'w_uk': nrm((L, MLA_KV_LORA, MLA_HEADS, MLA_NOPE), MLA_KV_LORA ** -0.5),
        'w_uv': nrm((L, MLA_KV_LORA, MLA_HEADS, MLA_V), MLA_KV_LORA ** -0.5),
        'lam_q1': nrm((L, DIFF_HD), 0.1),
        'lam_k1': nrm((L, DIFF_HD), 0.1),
        'lam_q2': nrm((L, DIFF_HD), 0.1),
        'lam_k2': nrm((L, DIFF_HD), 0.1),
        'g_diff': 1.0 + nrm((L, 2 * DIFF_HD), 0.05),
        'w_out': nrm((L, MIX_WIDTH, D), MIX_WIDTH ** -0.5),
        'w_router': nrm((L, D, E), D ** -0.5),
        'b_router': nrm((L, E), 0.01),
        'w_gate_up': nrm((L, E, D, 2 * F), D ** -0.5),
        'b_gate_up': nrm((L, E, 2 * F), 0.01),
        'w_down': nrm((L, E, F, D), F ** -0.5),
        'b_down': nrm((L, E, D), 0.01),
        'g_final': 1.0 + nrm((D,), 0.05),
    }


def reference(x_prompt, x_sample, c_prompt, c_sample, cache_mla_latent, cache_mla_krope,
              cache_diff_k, cache_diff_v, w_ada, b_ada, g_attn, g_ffn, w_in, g_q, w_uq, g_kv,
              w_uk, w_uv, lam_q1, lam_k1, lam_q2, lam_k2, g_diff, w_out, w_router, b_router,
              w_gate_up, b_gate_up, w_down, b_down, g_final):
    prm = {'w_ada': w_ada, 'b_ada': b_ada, 'g_attn': g_attn, 'g_ffn': g_ffn, 'w_in': w_in,
           'g_q': g_q, 'w_uq': w_uq, 'g_kv': g_kv, 'w_uk': w_uk, 'w_uv': w_uv,
           'lam_q1': lam_q1, 'lam_k1': lam_k1, 'lam_q2': lam_q2, 'lam_k2': lam_k2,
           'g_diff': g_diff, 'w_out': w_out, 'w_router': w_router, 'b_router': b_router,
           'w_gate_up': w_gate_up, 'b_gate_up': b_gate_up, 'w_down': w_down,
           'b_down': b_down, 'g_final': g_final}
    pos_p = jnp.arange(x_prompt.shape[1], dtype=jnp.int32)
    y_prompt, rows_p = trunk(x_prompt, c_prompt, pos_p, None, prm)
    past_len = cache_mla_latent.shape[2]
    pos_s = past_len + jnp.arange(x_sample.shape[1], dtype=jnp.int32)
    caches = (cache_mla_latent, cache_mla_krope, cache_diff_k, cache_diff_v)
    y_sample, rows_s = trunk(x_sample, c_sample, pos_s, caches, prm)
    lat_p, kr_p, dk_p, dv_p = rows_p
    lat_s, kr_s, dk_s, dv_s = rows_s
    return (y_prompt, y_sample, lat_p, kr_p, dk_p, dv_p, lat_s, kr_s, dk_s, dv_s)
```

```python
import functools
import math

import jax
import jax.numpy as jnp
from jax import lax
from jax.experimental import pallas as pl
from jax.experimental.pallas import tpu as pltpu

F32 = jnp.float32
BF16 = jnp.bfloat16
I32 = jnp.int32
U32 = jnp.uint32

CHUNK = 64
ROPE_THETA = 10000.0
EPS = 1e-6
MLA_HEADS = 8
MLA_Q_LORA = 256
MLA_KV_LORA = 128
MLA_NOPE = 64
MLA_ROPE = 32
MLA_V = 64
DIFF_HEADS = 4
DIFF_HD = 64
DIFF_QKV = DIFF_HEADS * 2 * DIFF_HD
N_EXPERTS = 32
TOP_K = 4
SWIGLU_LIMIT = 7.0
SWIGLU_ALPHA = 1.702

LANES = 128
MLA_PAD = MLA_HEADS * LANES
PROJ_COLS = MLA_Q_LORA + MLA_KV_LORA + LANES + 3 * DIFF_QKV
VMEM_LIMIT = 56 * 1024 * 1024
MOE_TILE = 256
TOK_TILE = 512
ATTN_TQ = 512
ATTN_TK = 512
RANK_TILE = 512
DISPATCH_TILE = 256
COMBINE_TILE = 128
NEG_BIG = -1e30
assert CHUNK & (CHUNK - 1) == 0
CHUNK_SHIFT = CHUNK.bit_length() - 1


def _cparams(sem):
    return pltpu.CompilerParams(dimension_semantics=sem, vmem_limit_bytes=VMEM_LIMIT)


def _rms(x, g):
    return x * lax.rsqrt(jnp.mean(x * x, axis=-1, keepdims=True) + EPS) * g


def _rope_lanes(v, cos, sin_signed, half):
    lane = lax.broadcasted_iota(I32, v.shape, 1)
    first = (lane % (2 * half)) < half
    swapped = jnp.where(first, pltpu.roll(v, LANES - half, 1), pltpu.roll(v, half, 1))
    return v * cos + swapped * sin_signed


def _ada_kernel(c_ref, w_ref, b_ref, o_ref):
    c = c_ref[...]
    s = c * jax.nn.sigmoid(c)
    o_ref[0] = jnp.dot(s.astype(BF16), w_ref[0], preferred_element_type=F32) + b_ref[0]


def _ada(c, w_ada, b_ada):
    depth, d, n6 = w_ada.shape
    rows = c.shape[0]
    tn = 1536
    return pl.pallas_call(
        _ada_kernel,
        grid=(depth, n6 // tn),
        in_specs=[pl.BlockSpec((rows, d), lambda l, j: (0, 0)),
                  pl.BlockSpec((1, d, tn), lambda l, j: (l, 0, j)),
                  pl.BlockSpec((1, 1, tn), lambda l, j: (l, 0, j))],
        out_specs=pl.BlockSpec((1, rows, tn), lambda l, j: (l, 0, j)),
        out_shape=jax.ShapeDtypeStruct((depth, rows, n6), F32),
        compiler_params=_cparams(("arbitrary", "arbitrary")),
    )(c, w_ada, b_ada.reshape(depth, 1, n6))


def _proj_kernel(x_ref, mod_ref, ga_ref, win_ref, gq_ref, wuq_ref, gkv_ref,
                 cq_ref, sq_ref, cd_ref, sd_ref,
                 qm_ref, qd_ref, lat_ref, kr_ref, dk_ref, dv_ref):
    x = x_ref[0]
    mod = mod_ref[0]
    h = _rms(x, ga_ref[...]) * (1.0 + mod[1:2]) + mod[0:1]
    proj = jnp.dot(h.astype(BF16), win_ref[...], preferred_element_type=F32)
    o = 0
    cq = proj[:, o:o + MLA_Q_LORA]; o += MLA_Q_LORA
    ckv = proj[:, o:o + MLA_KV_LORA]; o += MLA_KV_LORA
    krp = proj[:, o:o + LANES]; o += LANES
    dq = proj[:, o:o + DIFF_QKV]; o += DIFF_QKV
    dk = proj[:, o:o + DIFF_QKV]; o += DIFF_QKV
    dv = proj[:, o:o + DIFF_QKV]

    cos_q, sin_q = cq_ref[...], sq_ref[...]
    cos_d, sin_d = cd_ref[...], sd_ref[...]
    q = jnp.dot(_rms(cq, gq_ref[...]).astype(BF16), wuq_ref[...], preferred_element_type=F32)
    q_scale = 1.0 / math.sqrt(MLA_NOPE + MLA_ROPE)
    for hd in range(MLA_HEADS):
        sl = slice(hd * LANES, (hd + 1) * LANES)
        qm_ref[0, :, sl] = (_rope_lanes(q[:, sl], cos_q, sin_q, MLA_ROPE // 2) * q_scale).astype(BF16)
    lat_ref[0] = _rms(ckv, gkv_ref[...])
    kr_ref[0] = _rope_lanes(krp, cos_q, sin_q, MLA_ROPE // 2)[:, MLA_NOPE:MLA_NOPE + MLA_ROPE]
    d_scale = 1.0 / math.sqrt(DIFF_HD)
    for hd in range(DIFF_HEADS):
        sl = slice(hd * LANES, (hd + 1) * LANES)
        qd_ref[0, :, sl] = (_rope_lanes(dq[:, sl], cos_d, sin_d, DIFF_HD // 2) * d_scale).astype(BF16)
        dk_ref[0, :, sl] = _rope_lanes(dk[:, sl], cos_d, sin_d, DIFF_HD // 2)
    dv_ref[0] = dv


def _proj(x, mod, g_attn, w_in_p, g_q, w_uq_p, g_kv, tabs, t):
    b, s, d = x.shape
    cos_q, sin_q, cos_d, sin_d = tabs
    full = lambda shape: pl.BlockSpec(shape, lambda i, j: (0,) * len(shape))
    tok = lambda w: pl.BlockSpec((1, t, w), lambda i, j: (i, j, 0))
    tab = pl.BlockSpec((t, LANES), lambda i, j: (j, 0))
    return pl.pallas_call(
        _proj_kernel,
        grid=(b, s // t),
        in_specs=[tok(d), pl.BlockSpec((1, 6, d), lambda i, j: (i, 0, 0)), full((1, d)),
                  full((d, PROJ_COLS)), full((1, MLA_Q_LORA)), full((MLA_Q_LORA, MLA_PAD)),
                  full((1, MLA_KV_LORA)), tab, tab, tab, tab],
        out_specs=[tok(MLA_PAD), tok(DIFF_QKV), tok(MLA_KV_LORA), tok(MLA_ROPE), tok(DIFF_QKV), tok(DIFF_QKV)],
        out_shape=[jax.ShapeDtypeStruct((b, s, MLA_PAD), BF16),
                   jax.ShapeDtypeStruct((b, s, DIFF_QKV), BF16),
                   jax.ShapeDtypeStruct((b, s, MLA_KV_LORA), F32),
                   jax.ShapeDtypeStruct((b, s, MLA_ROPE), F32),
                   jax.ShapeDtypeStruct((b, s, DIFF_QKV), F32),
                   jax.ShapeDtypeStruct((b, s, DIFF_QKV), F32)],
        compiler_params=_cparams(("arbitrary", "arbitrary")),
    )(x, mod, g_attn, w_in_p, g_q, w_uq_p, g_kv, cos_q, sin_q, cos_d, sin_d)


def _kv_kernel(lat_ref, kr_ref, dk_ref, dv_ref, wukv_ref, place_ref, km_ref, vm_ref, dkb_ref, dvb_ref):
    kv = jnp.dot(lat_ref[0].astype(BF16), wukv_ref[...], preferred_element_type=F32)
    kv = kv + jnp.dot(kr_ref[0].astype(BF16), place_ref[...], preferred_element_type=F32)
    km_ref[0] = kv[:, :MLA_PAD].astype(BF16)
    vm_ref[0] = kv[:, MLA_PAD:].astype(BF16)
    dkb_ref[0] = dk_ref[0].astype(BF16)
    dvb_ref[0] = dv_ref[0].astype(BF16)


def _kv(lat, kr, dk, dv, w_ukv_p, place, t):
    b, s, _ = lat.shape
    full = lambda shape: pl.BlockSpec(shape, lambda i, j: (0,) * len(shape))
    tok = lambda w: pl.BlockSpec((1, t, w), lambda i, j: (i, j, 0))
    return pl.pallas_call(
        _kv_kernel,
        grid=(b, s // t),
        in_specs=[tok(MLA_KV_LORA), tok(MLA_ROPE), tok(DIFF_QKV), tok(DIFF_QKV),
                  full((MLA_KV_LORA, 2 * MLA_PAD)), full((MLA_ROPE, 2 * MLA_PAD))],
        out_specs=[tok(MLA_PAD), tok(MLA_PAD), tok(DIFF_QKV), tok(DIFF_QKV)],
        out_shape=[jax.ShapeDtypeStruct((b, s, MLA_PAD), BF16), jax.ShapeDtypeStruct((b, s, MLA_PAD), BF16),
                   jax.ShapeDtypeStruct((b, s, DIFF_QKV), BF16), jax.ShapeDtypeStruct((b, s, DIFF_QKV), BF16)],
        compiler_params=_cparams(("arbitrary", "arbitrary")),
    )(lat, kr, dk, dv, w_ukv_p, place)


def _attn_kernel(*refs, tq, tk, nk, q_off, diff, lam_init):
    if diff:
        (q_ref, k_ref, v_ref, lq1_ref, lk1_ref, lq2_ref, lk2_ref, gd_ref,
         o_ref, m_ref, l_ref, acc_ref, qs_ref) = refs
    else:
        q_ref, k_ref, v_ref, o_ref, m_ref, l_ref, acc_ref = refs
    qi = pl.program_id(2)
    ki = pl.program_id(3)
    rows = 2 * tq if diff else tq

    @pl.when(ki == 0)
    def _():
        m_ref[...] = jnp.full(m_ref.shape, NEG_BIG, F32)
        l_ref[...] = jnp.zeros(l_ref.shape, F32)
        acc_ref[...] = jnp.zeros(acc_ref.shape, F32)
        if diff:
            q = q_ref[0]
            lane = lax.broadcasted_iota(I32, q.shape, 1)
            qs_ref[0:tq, :] = jnp.where(lane < DIFF_HD, q, jnp.zeros_like(q))
            qs_ref[tq:, :] = jnp.where(lane >= DIFF_HD, q, jnp.zeros_like(q))

    q_first = (q_off + qi * tq) >> CHUNK_SHIFT
    q_last = (q_off + qi * tq + tq - 1) >> CHUNK_SHIFT
    k_first = (ki * tk) >> CHUNK_SHIFT
    k_last = (ki * tk + tk - 1) >> CHUNK_SHIFT

    def step(masked):
        q = qs_ref[...] if diff else q_ref[0]
        s = lax.dot_general(q, k_ref[0], (((1,), (1,)), ((), ())), preferred_element_type=F32)
        if masked:
            r = lax.broadcasted_iota(I32, (rows, 1), 0)
            if diff:
                r = jnp.where(r >= tq, r - tq, r)
            q_chunk = (q_off + qi * tq + r) >> CHUNK_SHIFT
            k_chunk = (ki * tk + lax.broadcasted_iota(I32, (1, tk), 1)) >> CHUNK_SHIFT
            s = jnp.where(k_chunk <= q_chunk, s, -jnp.inf)
        m_prev = m_ref[...]
        m_new = jnp.maximum(m_prev, jnp.max(s, axis=1, keepdims=True))
        alpha = jnp.exp(m_prev - m_new)
        p = jnp.exp(s - m_new)
        l_ref[...] = alpha * l_ref[...] + jnp.sum(p, axis=1, keepdims=True)
        acc_ref[...] = alpha * acc_ref[...] + jnp.dot(p.astype(BF16), v_ref[0], preferred_element_type=F32)
        m_ref[...] = m_new

    visible = k_first <= q_last
    unmasked = k_last <= q_first

    @pl.when(jnp.logical_and(visible, unmasked))
    def _():
        step(False)

    @pl.when(jnp.logical_and(visible, jnp.logical_not(unmasked)))
    def _():
        step(True)

    @pl.when(ki == nk - 1)
    def _():
        o = acc_ref[...] / l_ref[...]
        if diff:
            lam = (jnp.exp(jnp.sum(lq1_ref[...] * lk1_ref[...], keepdims=True))
                   - jnp.exp(jnp.sum(lq2_ref[...] * lk2_ref[...], keepdims=True)) + lam_init)
            od = o[0:tq] - lam * o[tq:]
            o_ref[0] = (_rms(od, gd_ref[...]) * (1.0 - lam_init)).astype(o_ref.dtype)
        else:
            o_ref[0] = o.astype(o_ref.dtype)


def _attention(q, k, v, heads, tq, tk, q_off, diff_args=None, lam_init=0.0):
    b, sq, _ = q.shape
    sk = k.shape[1]
    nq, nk = sq // tq, sk // tk
    diff = diff_args is not None

    def k_map(i, h, qi, ki):
        last_needed = ((q_off + qi * tq + tq - 1) // CHUNK * CHUNK + CHUNK - 1) // tk
        return (i, jnp.minimum(ki, jnp.minimum(last_needed, nk - 1)), h)

    in_specs = [pl.BlockSpec((1, tq, LANES), lambda i, h, qi, ki: (i, qi, h)),
                pl.BlockSpec((1, tk, LANES), k_map),
                pl.BlockSpec((1, tk, LANES), k_map)]
    rows = 2 * tq if diff else tq
    scratch = [pltpu.VMEM((rows, 1), F32), pltpu.VMEM((rows, 1), F32), pltpu.VMEM((rows, LANES), F32)]
    args = [q, k, v]
    if diff:
        small = lambda w: pl.BlockSpec((1, w), lambda i, h, qi, ki: (0, 0))
        in_specs += [small(DIFF_HD)] * 4 + [small(2 * DIFF_HD)]
        scratch.append(pltpu.VMEM((rows, LANES), BF16))
        args += list(diff_args)
    kern = functools.partial(_attn_kernel, tq=tq, tk=tk, nk=nk, q_off=q_off, diff=diff, lam_init=lam_init)
    return pl.pallas_call(
        kern,
        grid=(b, heads, nq, nk),
        in_specs=in_specs,
        out_specs=pl.BlockSpec((1, tq, LANES), lambda i, h, qi, ki: (i, qi, h)),
        out_shape=jax.ShapeDtypeStruct((b, sq, heads * LANES), BF16),
        scratch_shapes=scratch,
        compiler_params=_cparams(("arbitrary",) * 4),
    )(*args)


def _out_kernel(x_ref, om_ref, od_ref, wout_ref, mod_ref, gf_ref, wr_ref, br_ref,
                x1_ref, hp_ref, idx_ref, gate_ref):
    a = (jnp.dot(om_ref[0], wout_ref[0:MLA_PAD, :], preferred_element_type=F32)
         + jnp.dot(od_ref[0], wout_ref[MLA_PAD:, :], preferred_element_type=F32))
    mod = mod_ref[0]
    x1 = x_ref[0] + mod[2:3] * a
    x1_ref[0] = x1
    h2 = _rms(x1, gf_ref[...]) * (1.0 + mod[4:5]) + mod[3:4]
    bits = lax.bitcast_convert_type(h2.astype(BF16).astype(F32), U32)
    half = h2.shape[1] // 2
    hp_ref[0] = (bits[:, :half] >> 16) | (bits[:, half:] & jnp.uint32(0xFFFF0000))

    logits = lax.dot_general(wr_ref[...], h2, (((1,), (1,)), ((), ())), precision=lax.Precision.HIGHEST,
                             preferred_element_type=F32) + br_ref[...]
    e_iota = lax.broadcasted_iota(I32, logits.shape, 0).astype(F32)
    vals, idxs = [], []
    for _ in range(TOP_K):
        m = jnp.max(logits, axis=0, keepdims=True)
        i = jnp.min(jnp.where(logits == m, e_iota, float(N_EXPERTS)), axis=0, keepdims=True)
        vals.append(m)
        idxs.append(i)
        logits = jnp.where(e_iota == i, -jnp.inf, logits)
    ex = [jnp.exp(v - vals[0]) for v in vals]
    tot = ex[0] + ex[1] + ex[2] + ex[3]
    idx_ref[0] = jnp.concatenate(idxs, axis=0).astype(I32)
    gate_ref[0] = jnp.concatenate([e / tot for e in ex], axis=0)


def _out(x, om, od, w_out_p, mod, g_ffn, w_r, b_r, t):
    b, s, d = x.shape
    ns = s // t
    full = lambda shape: pl.BlockSpec(shape, lambda i, j: (0,) * len(shape))
    tok = lambda w: pl.BlockSpec((1, t, w), lambda i, j: (i, j, 0))
    flat = pl.BlockSpec((1, TOP_K, t), lambda i, j: (i * ns + j, 0, 0))
    x1, hp, idx, gates = pl.pallas_call(
        _out_kernel,
        grid=(b, ns),
        in_specs=[tok(d), tok(MLA_PAD), tok(DIFF_QKV), full((MLA_PAD + DIFF_QKV, d)),
                  pl.BlockSpec((1, 6, d), lambda i, j: (i, 0, 0)), full((1, d)),
                  full((N_EXPERTS, d)), full((N_EXPERTS, 1))],
        out_specs=[tok(d), tok(d // 2), flat, flat],
        out_shape=[jax.ShapeDtypeStruct((b, s, d), F32), jax.ShapeDtypeStruct((b, s, d // 2), U32),
                   jax.ShapeDtypeStruct((b * ns, TOP_K, t), I32), jax.ShapeDtypeStruct((b * ns, TOP_K, t), F32)],
        compiler_params=_cparams(("arbitrary", "arbitrary")),
    )(x, om, od, w_out_p, mod, g_ffn, w_r, b_r)
    unflat = lambda a: a.transpose(1, 0, 2).reshape(TOP_K, b * s)
    return x1, hp, unflat(idx), unflat(gates)


def _rank_kernel(idx_ref, rank_ref, cnt_ref, carry_ref):
    step = pl.program_id(0)

    @pl.when(step == 0)
    def _():
        carry_ref[...] = jnp.zeros(carry_ref.shape, F32)

    idx = idx_ref[...]
    t = idx.shape[1]
    e_iota = lax.broadcasted_iota(I32, (N_EXPERTS, t), 0)
    hits = [e_iota == idx[k:k + 1, :] for k in range(TOP_K)]
    onehot = sum(h.astype(F32) for h in hits)
    earlier = (lax.broadcasted_iota(I32, (t, t), 0) < lax.broadcasted_iota(I32, (t, t), 1)).astype(BF16)
    before = jnp.dot(onehot.astype(BF16), earlier, preferred_element_type=F32) + carry_ref[...]
    rank_ref[...] = jnp.concatenate(
        [jnp.sum(jnp.where(h, before, 0.0), axis=0, keepdims=True) for h in hits], axis=0).astype(I32)
    carry_ref[...] = carry_ref[...] + jnp.sum(onehot, axis=1, keepdims=True)
    cnt_ref[...] = carry_ref[...].astype(I32)


def _rank(idx, t):
    n = idx.shape[1]
    return pl.pallas_call(
        _rank_kernel,
        grid=(n // t,),
        in_specs=[pl.BlockSpec((TOP_K, t), lambda i: (0, i))],
        out_specs=[pl.BlockSpec((TOP_K, t), lambda i: (0, i)), pl.BlockSpec((N_EXPERTS, 1), lambda i: (0, 0))],
        out_shape=[jax.ShapeDtypeStruct((TOP_K, n), I32), jax.ShapeDtypeStruct((N_EXPERTS, 1), I32)],
        scratch_shapes=[pltpu.VMEM((N_EXPERTS, 1), F32)],
        compiler_params=_cparams(("arbitrary",)),
    )(idx)


def _dispatch_kernel(dest_ref, hp_ref, xs_in_ref, xs_ref, dest_smem, sem, idx_sem, *, t):
    del xs_in_ref
    cp = pltpu.make_async_copy(dest_ref.at[0], dest_smem, idx_sem)
    cp.start()
    cp.wait()

    def row_copy(tok, k):
        return pltpu.make_async_copy(hp_ref.at[pl.ds(tok, 1)], xs_ref.at[pl.ds(dest_smem[k, tok], 1)], sem)

    def issue(tok, c):
        for k in range(TOP_K):
            row_copy(tok, k).start()
        return c

    def drain(tok, c):
        for k in range(TOP_K):
            row_copy(tok, k).wait()
        return c

    lax.fori_loop(0, t, issue, 0)
    lax.fori_loop(0, t, drain, 0)


def _dispatch(dest3, hp, n_slots, t):
    n, w = hp.shape
    zeros = jnp.zeros((n_slots, w), U32)
    return pl.pallas_call(
        functools.partial(_dispatch_kernel, t=t),
        grid=(n // t,),
        in_specs=[pl.BlockSpec((1, TOP_K, t), lambda i: (i, 0, 0)),
                  pl.BlockSpec((t, w), lambda i: (i, 0)),
                  pl.BlockSpec(memory_space=pl.ANY)],
        out_specs=pl.BlockSpec(memory_space=pl.ANY),
        out_shape=jax.ShapeDtypeStruct((n_slots, w), U32),
        scratch_shapes=[pltpu.SMEM((TOP_K, t), I32), pltpu.SemaphoreType.DMA, pltpu.SemaphoreType.DMA],
        input_output_aliases={2: 0},
        compiler_params=_cparams(("arbitrary",)),
    )(dest3, hp, zeros)


def _expert_kernel(te_ref, nu_ref, xs_ref, wgu_ref, bgu_ref, wd_ref, bd_ref, ys_ref):
    del te_ref

    @pl.when(pl.program_id(0) < nu_ref[0])
    def _():
        p = xs_ref[...]
        x_lo = lax.bitcast_convert_type(p << 16, F32).astype(BF16)
        x_hi = lax.bitcast_convert_type(p & jnp.uint32(0xFFFF0000), F32).astype(BF16)
        half = p.shape[1]
        gu = (jnp.dot(x_lo, wgu_ref[0, 0:half, :], preferred_element_type=F32)
              + jnp.dot(x_hi, wgu_ref[0, half:, :], preferred_element_type=F32) + bgu_ref[0])
        f = gu.shape[1] // 2
        glu = jnp.minimum(gu[:, :f], SWIGLU_LIMIT)
        lin = jnp.clip(gu[:, f:], -SWIGLU_LIMIT, SWIGLU_LIMIT)
        act = glu * jax.nn.sigmoid(SWIGLU_ALPHA * glu) * (lin + 1.0)
        ys_ref[...] = jnp.dot(act.astype(BF16), wd_ref[0], preferred_element_type=F32) + bd_ref[0]


def _experts(tile_e, n_used, xs, w_gu, b_gu, w_down, b_down):
    n_slots, half = xs.shape
    e, d, f2 = w_gu.shape
    n_tiles = n_slots // MOE_TILE
    row = lambda i, te, nu: (jnp.minimum(i, nu[0] - 1), 0)
    wsel = lambda i, te, nu: (te[i], 0, 0)
    grid_spec = pltpu.PrefetchScalarGridSpec(
        num_scalar_prefetch=2,
        grid=(n_tiles,),
        in_specs=[pl.BlockSpec((MOE_TILE, half), row),
                  pl.BlockSpec((1, d, f2), wsel), pl.BlockSpec((1, 1, f2), wsel),
                  pl.BlockSpec((1, f2 // 2, d), wsel), pl.BlockSpec((1, 1, d), wsel)],
        out_specs=pl.BlockSpec((MOE_TILE, d), row),
    )
    return pl.pallas_call(
        _expert_kernel,
        grid_spec=grid_spec,
        out_shape=jax.ShapeDtypeStruct((n_slots, d), F32),
        compiler_params=_cparams(("arbitrary",)),
    )(tile_e, n_used, xs, w_gu, b_gu.reshape(e, 1, f2), w_down, b_down.reshape(e, 1, d))


def _combine_kernel(dcur_ref, dnxt_ref, gate_ref, x1_ref, mod_ref, gfin_ref, ys_ref,
                    x2_ref, yn_ref, buf, dsm, sem, idx_sem, *, t, n_steps):
    i = pl.program_id(0)
    slot = i % 2

    def load_idx(src, s):
        cp = pltpu.make_async_copy(src.at[0], dsm.at[s], idx_sem)
        cp.start()
        cp.wait()

    def row_copy(s, tok, k):
        return pltpu.make_async_copy(ys_ref.at[pl.ds(dsm[s, k, tok], 1)], buf.at[s, k, pl.ds(tok, 1)], sem.at[s])

    def issue(s):
        def body(tok, c):
            for k in range(TOP_K):
                row_copy(s, tok, k).start()
            return c
        lax.fori_loop(0, t, body, 0)

    @pl.when(i == 0)
    def _():
        load_idx(dcur_ref, 0)
        issue(0)

    @pl.when(i + 1 < n_steps)
    def _():
        load_idx(dnxt_ref, 1 - slot)
        issue(1 - slot)

    def drain(tok, c):
        for k in range(TOP_K):
            row_copy(slot, tok, k).wait()
        return c
    lax.fori_loop(0, t, drain, 0)

    g = gate_ref[...]
    y = g[:, 0:1] * buf[slot, 0]
    for k in range(1, TOP_K):
        y = y + g[:, k:k + 1] * buf[slot, k]
    x2 = x1_ref[...] + mod_ref[0][5:6] * y
    x2_ref[...] = x2
    yn_ref[...] = _rms(x2, gfin_ref[...])


def _combine(dest3, gates_t, x1, mod, g_final, ys, s, t):
    n, d = x1.shape
    n_steps = n // t
    per_b = s // t
    return pl.pallas_call(
        functools.partial(_combine_kernel, t=t, n_steps=n_steps),
        grid=(n_steps,),
        in_specs=[pl.BlockSpec((1, TOP_K, t), lambda i: (i, 0, 0)),
                  pl.BlockSpec((1, TOP_K, t), lambda i: (jnp.minimum(i + 1, n_steps - 1), 0, 0)),
                  pl.BlockSpec((t, TOP_K), lambda i: (i, 0)),
                  pl.BlockSpec((t, d), lambda i: (i, 0)),
                  pl.BlockSpec((1, 6, d), lambda i: (i // per_b, 0, 0)),
                  pl.BlockSpec((1, d), lambda i: (0, 0)),
                  pl.BlockSpec(memory_space=pl.ANY)],
        out_specs=[pl.BlockSpec((t, d), lambda i: (i, 0)), pl.BlockSpec((t, d), lambda i: (i, 0))],
        out_shape=[jax.ShapeDtypeStruct((n, d), F32), jax.ShapeDtypeStruct((n, d), F32)],
        scratch_shapes=[pltpu.VMEM((2, TOP_K, t, d), F32), pltpu.SMEM((2, TOP_K, t), I32),
                        pltpu.SemaphoreType.DMA((2,)), pltpu.SemaphoreType.DMA],
        compiler_params=_cparams(("arbitrary",)),
    )(dest3, dest3, gates_t, x1, mod, g_final, ys)


def _rope_tables(pos, d, lane_pos):
    inv = ROPE_THETA ** (-jnp.arange(0, d, 2, dtype=F32) / d)
    ang = pos.astype(F32)[:, None] * inv[None, :]
    cos, sin = jnp.cos(ang), jnp.sin(ang)
    cos_t = jnp.ones((pos.shape[0], LANES), F32)
    sin_t = jnp.zeros((pos.shape[0], LANES), F32)
    for o in lane_pos:
        cos_t = cos_t.at[:, o:o + d].set(jnp.concatenate([cos, cos], axis=1))
        sin_t = sin_t.at[:, o:o + d].set(jnp.concatenate([-sin, sin], axis=1))
    return cos_t, sin_t


def _prep_weights(w_in, w_uq, w_uk, w_uv, w_out):
    depth, d, _ = w_in.shape
    cuts = [MLA_Q_LORA, MLA_KV_LORA, MLA_ROPE, DIFF_QKV, DIFF_QKV, DIFF_QKV]
    offs = [0]
    for c in cuts:
        offs.append(offs[-1] + c)
    parts = [w_in[:, :, offs[i]:offs[i + 1]] for i in range(6)]
    kr_pad = jnp.pad(parts[2], ((0, 0), (0, 0), (MLA_NOPE, LANES - MLA_NOPE - MLA_ROPE)))
    w_in_p = jnp.concatenate([parts[0], parts[1], kr_pad, parts[3], parts[4], parts[5]], axis=2).astype(BF16)
    qd = MLA_NOPE + MLA_ROPE
    w_uq_p = jnp.pad(w_uq.reshape(depth, MLA_Q_LORA, MLA_HEADS, qd), ((0, 0), (0, 0), (0, 0), (0, LANES - qd)))
    w_uq_p = w_uq_p.reshape(depth, MLA_Q_LORA, MLA_PAD).astype(BF16)
    pad_head = lambda w, n: jnp.pad(w, ((0, 0), (0, 0), (0, 0), (0, LANES - n))).reshape(depth, MLA_KV_LORA, MLA_PAD)
    w_ukv_p = jnp.concatenate([pad_head(w_uk, MLA_NOPE), pad_head(w_uv, MLA_V)], axis=2).astype(BF16)
    wo_m = jnp.pad(w_out[:, :MLA_HEADS * MLA_V].reshape(depth, MLA_HEADS, MLA_V, d),
                   ((0, 0), (0, 0), (0, LANES - MLA_V), (0, 0))).reshape(depth, MLA_PAD, d)
    w_out_p = jnp.concatenate([wo_m, w_out[:, MLA_HEADS * MLA_V:]], axis=1).astype(BF16)
    place = jnp.zeros((MLA_ROPE, 2 * MLA_PAD), F32)
    eye = jnp.eye(MLA_ROPE, dtype=F32)
    for hd in range(MLA_HEADS):
        place = place.at[:, hd * LANES + MLA_NOPE:hd * LANES + MLA_NOPE + MLA_ROPE].set(eye)
    return w_in_p, w_uq_p, w_ukv_p, w_out_p, place.astype(BF16)


def _pick(n, pref):
    t = min(n, pref)
    while n % t:
        t //= 2
    return t


def _moe(hp, idx, gates, x1, mod_l, g_final, wts, s):
    w_gu, b_gu, w_down, b_down = wts
    n = hp.shape[0]
    rank, counts = _rank(idx, _pick(n, RANK_TILE))
    counts = counts[:, 0]
    padded = (counts + MOE_TILE - 1) // MOE_TILE * MOE_TILE
    pad_ends = jnp.cumsum(padded)
    pad_starts = pad_ends - padded
    dest = rank + jnp.sum(jnp.where(idx[None] == jnp.arange(N_EXPERTS, dtype=I32)[:, None, None],
                                    pad_starts[:, None, None], 0), axis=0)
    n_tiles = (n * TOP_K) // MOE_TILE + N_EXPERTS
    tile_e = jnp.minimum(jnp.sum(pad_ends[None, :] <= (jnp.arange(n_tiles, dtype=I32) * MOE_TILE)[:, None], axis=1),
                         N_EXPERTS - 1).astype(I32)
    n_used = (pad_ends[-1:] // MOE_TILE).astype(I32)
    td = _pick(n, DISPATCH_TILE)
    xs = _dispatch(dest.reshape(TOP_K, n // td, td).transpose(1, 0, 2), hp, n_tiles * MOE_TILE, td)
    ys = _experts(tile_e, n_used, xs, w_gu, b_gu, w_down, b_down)
    tc = _pick(s, COMBINE_TILE)
    return _combine(dest.reshape(TOP_K, n // tc, tc).transpose(1, 0, 2), gates.T, x1, mod_l, g_final, ys, s, tc)


def _trunk(x, c, caches, prm, prepped, q_off):
    b, s, d = x.shape
    depth = prm['w_in'].shape[0]
    w_in_p, w_uq_p, w_ukv_p, w_out_p, place, w_ada_b, w_gu_b, w_down_b = prepped
    rows = -(-b // 16) * 16
    mod = _ada(jnp.pad(c, ((0, rows - b), (0, 0))), w_ada_b, prm['b_ada'])[:, :b].reshape(depth, b, 6, d)
    pos = q_off + jnp.arange(s, dtype=I32)
    cos_q, sin_q = _rope_tables(pos, MLA_ROPE, (MLA_NOPE,))
    cos_d, sin_d = _rope_tables(pos, DIFF_HD, (0, DIFF_HD))
    t = _pick(s, TOK_TILE)
    new_rows = ([], [], [], [])
    row = lambda v: v.reshape(1, -1)
    yn = None
    for l in range(depth):
        qm, qd, lat, kr, dk, dv = _proj(x, mod[l], row(prm['g_attn'][l]), w_in_p[l], row(prm['g_q'][l]), w_uq_p[l],
                                        row(prm['g_kv'][l]), (cos_q, sin_q, cos_d, sin_d), t)
        for lst, r in zip(new_rows, (lat, kr, dk, dv)):
            lst.append(r)
        if caches is not None:
            lat_a, kr_a, dk_a, dv_a = (jnp.concatenate([cc[l].reshape(b, cc.shape[2], -1), r], axis=1)
                                       for cc, r in zip(caches, (lat, kr, dk, dv)))
        else:
            lat_a, kr_a, dk_a, dv_a = lat, kr, dk, dv
        sk = lat_a.shape[1]
        tk = ATTN_TK if sk % ATTN_TK == 0 else sk
        tkv = max(t_ for t_ in range(16, min(sk, 2 * TOK_TILE) + 1, 16) if sk % t_ == 0)
        km, vm, dkb, dvb = _kv(lat_a, kr_a, dk_a, dv_a, w_ukv_p[l], place, tkv)
        lam_init = 0.8 - 0.6 * math.exp(-0.3 * l)
        om = _attention(qm, km, vm, MLA_HEADS, _pick(s, ATTN_TQ), tk, q_off)
        od = _attention(qd, dkb, dvb, DIFF_HEADS, _pick(s, ATTN_TQ // 2), tk, q_off,
                        diff_args=(row(prm['lam_q1'][l]), row(prm['lam_k1'][l]), row(prm['lam_q2'][l]),
                                   row(prm['lam_k2'][l]), row(prm['g_diff'][l])), lam_init=lam_init)
        x1, hp, idx, gates = _out(x, om, od, w_out_p[l], mod[l], row(prm['g_ffn'][l]),
                                  prm['w_router'][l].T, prm['b_router'][l].reshape(-1, 1), t)
        x2, yn = _moe(hp.reshape(b * s, d // 2), idx, gates, x1.reshape(b * s, d), mod[l], row(prm['g_final']),
                      (w_gu_b[l], prm['b_gate_up'][l], w_down_b[l], prm['b_down'][l]), s)
        x = x2.reshape(b, s, d)
    lat_n, kr_n, dk_n, dv_n = (jnp.stack(lst) for lst in new_rows)
    shp = (depth, b, s, DIFF_HEADS, 2 * DIFF_HD)
    return yn.reshape(b, s, d), (lat_n, kr_n, dk_n.reshape(shp), dv_n.reshape(shp))


def kernel(x_prompt, x_sample, c_prompt, c_sample, cache_mla_latent, cache_mla_krope, cache_diff_k, cache_diff_v,
           w_ada, b_ada, g_attn, g_ffn, w_in, g_q, w_uq, g_kv, w_uk, w_uv, lam_q1, lam_k1, lam_q2, lam_k2, g_diff,
           w_out, w_router, b_router, w_gate_up, b_gate_up, w_down, b_down, g_final):
    prm = {'b_ada': b_ada, 'g_attn': g_attn, 'g_ffn': g_ffn, 'w_in': w_in, 'g_q': g_q, 'g_kv': g_kv,
           'lam_q1': lam_q1, 'lam_k1': lam_k1, 'lam_q2': lam_q2, 'lam_k2': lam_k2, 'g_diff': g_diff,
           'w_router': w_router, 'b_router': b_router, 'b_gate_up': b_gate_up, 'b_down': b_down,
           'g_final': g_final}
    prepped = _prep_weights(w_in, w_uq, w_uk, w_uv, w_out) + (
        w_ada.astype(BF16), w_gate_up.astype(BF16), w_down.astype(BF16))
    y_p, rows_p = _trunk(x_prompt, c_prompt, None, prm, prepped, 0)
    caches = (cache_mla_latent, cache_mla_krope, cache_diff_k, cache_diff_v)
    y_s, rows_s = _trunk(x_sample, c_sample, caches, prm, prepped, cache_mla_latent.shape[2])
    return (y_p, y_s) + rows_p + rows_s
```

```python
import functools
import math

import jax
import jax.numpy as jnp
from jax import lax
from jax.experimental import pallas as pl
from jax.experimental.pallas import tpu as pltpu

F32 = jnp.float32
BF16 = jnp.bfloat16
I32 = jnp.int32
U32 = jnp.uint32

CHUNK = 64
ROPE_THETA = 10000.0
EPS = 1e-6
MLA_HEADS = 8
MLA_Q_LORA = 256
MLA_KV_LORA = 128
MLA_NOPE = 64
MLA_ROPE = 32
MLA_V = 64
DIFF_HEADS = 4
DIFF_HD = 64
DIFF_QKV = DIFF_HEADS * 2 * DIFF_HD
N_EXPERTS = 32
TOP_K = 4
SWIGLU_LIMIT = 7.0
SWIGLU_ALPHA = 1.702

LANES = 128
MLA_PAD = MLA_HEADS * LANES
PROJ_COLS = MLA_Q_LORA + MLA_KV_LORA + LANES + 3 * DIFF_QKV
VMEM_LIMIT = 56 * 1024 * 1024
MOE_TILE = 256
TOK_TILE = 512
ATTN_TQ = 512
ATTN_TK = 1024
RANK_TILE = 512
DISPATCH_TILE = 256
COMBINE_TILE = 128
ISSUE_UNROLL = 4
NEG_BIG = -1e30
LOG2E = math.log2(math.e)
assert CHUNK & (CHUNK - 1) == 0
CHUNK_SHIFT = CHUNK.bit_length() - 1


def _cparams(sem):
    return pltpu.CompilerParams(dimension_semantics=sem, vmem_limit_bytes=VMEM_LIMIT)


def _rms(x, g):
    return x * lax.rsqrt(jnp.mean(x * x, axis=-1, keepdims=True) + EPS) * g


def _rope_lanes(v, cos, sin_signed, half):
    lane = lax.broadcasted_iota(I32, v.shape, 1)
    first = (lane % (2 * half)) < half
    swapped = jnp.where(first, pltpu.roll(v, LANES - half, 1), pltpu.roll(v, half, 1))
    return v * cos + swapped * sin_signed


def _ada_kernel(c_ref, w_ref, b_ref, o_ref):
    c = c_ref[...]
    s = c * jax.nn.sigmoid(c)
    o_ref[0] = jnp.dot(s.astype(BF16), w_ref[0], preferred_element_type=F32) + b_ref[0]


def _ada(c, w_ada, b_ada):
    depth, d, n6 = w_ada.shape
    rows = c.shape[0]
    tn = 1536
    return pl.pallas_call(
        _ada_kernel,
        grid=(depth, n6 // tn),
        in_specs=[pl.BlockSpec((rows, d), lambda l, j: (0, 0)),
                  pl.BlockSpec((1, d, tn), lambda l, j: (l, 0, j)),
                  pl.BlockSpec((1, 1, tn), lambda l, j: (l, 0, j))],
        out_specs=pl.BlockSpec((1, rows, tn), lambda l, j: (l, 0, j)),
        out_shape=jax.ShapeDtypeStruct((depth, rows, n6), F32),
        compiler_params=_cparams(("arbitrary", "arbitrary")),
    )(c, w_ada, b_ada.reshape(depth, 1, n6))


def _proj_kernel(x_ref, mod_ref, ga_ref, win_ref, gq_ref, wuq_ref, gkv_ref,
                 cq_ref, sq_ref, cd_ref, sd_ref,
                 qm_ref, qd_ref, lat_ref, kr_ref, dk_ref, dv_ref):
    x = x_ref[0]
    mod = mod_ref[0]
    h = _rms(x, ga_ref[...]) * (1.0 + mod[1:2]) + mod[0:1]
    proj = jnp.dot(h.astype(BF16), win_ref[...], preferred_element_type=F32)
    o = 0
    cq = proj[:, o:o + MLA_Q_LORA]; o += MLA_Q_LORA
    ckv = proj[:, o:o + MLA_KV_LORA]; o += MLA_KV_LORA
    krp = proj[:, o:o + LANES]; o += LANES
    dq = proj[:, o:o + DIFF_QKV]; o += DIFF_QKV
    dk = proj[:, o:o + DIFF_QKV]; o += DIFF_QKV
    dv = proj[:, o:o + DIFF_QKV]

    cos_q, sin_q = cq_ref[...], sq_ref[...]
    cos_d, sin_d = cd_ref[...], sd_ref[...]
    q = jnp.dot(_rms(cq, gq_ref[...]).astype(BF16), wuq_ref[...], preferred_element_type=F32)
    q_scale = LOG2E / math.sqrt(MLA_NOPE + MLA_ROPE)
    for hd in range(MLA_HEADS):
        sl = slice(hd * LANES, (hd + 1) * LANES)
        qm_ref[0, :, sl] = (_rope_lanes(q[:, sl], cos_q, sin_q, MLA_ROPE // 2) * q_scale).astype(BF16)
    lat_ref[0] = _rms(ckv, gkv_ref[...])
    kr_ref[0] = _rope_lanes(krp, cos_q, sin_q, MLA_ROPE // 2)[:, MLA_NOPE:MLA_NOPE + MLA_ROPE]
    d_scale = LOG2E / math.sqrt(DIFF_HD)
    for hd in range(DIFF_HEADS):
        sl = slice(hd * LANES, (hd + 1) * LANES)
        qd_ref[0, :, sl] = (_rope_lanes(dq[:, sl], cos_d, sin_d, DIFF_HD // 2) * d_scale).astype(BF16)
        dk_ref[0, :, sl] = _rope_lanes(dk[:, sl], cos_d, sin_d, DIFF_HD // 2)
    dv_ref[0] = dv


def _proj(x, mod, g_attn, w_in_p, g_q, w_uq_p, g_kv, tabs, t):
    b, s, d = x.shape
    cos_q, sin_q, cos_d, sin_d = tabs
    full = lambda shape: pl.BlockSpec(shape, lambda i, j: (0,) * len(shape))
    tok = lambda w: pl.BlockSpec((1, t, w), lambda i, j: (i, j, 0))
    tab = pl.BlockSpec((t, LANES), lambda i, j: (j, 0))
    return pl.pallas_call(
        _proj_kernel,
        grid=(b, s // t),
        in_specs=[tok(d), pl.BlockSpec((1, 6, d), lambda i, j: (i, 0, 0)), full((1, d)),
                  full((d, PROJ_COLS)), full((1, MLA_Q_LORA)), full((MLA_Q_LORA, MLA_PAD)),
                  full((1, MLA_KV_LORA)), tab, tab, tab, tab],
        out_specs=[tok(MLA_PAD), tok(DIFF_QKV), tok(MLA_KV_LORA), tok(MLA_ROPE), tok(DIFF_QKV), tok(DIFF_QKV)],
        out_shape=[jax.ShapeDtypeStruct((b, s, MLA_PAD), BF16),
                   jax.ShapeDtypeStruct((b, s, DIFF_QKV), BF16),
                   jax.ShapeDtypeStruct((b, s, MLA_KV_LORA), F32),
                   jax.ShapeDtypeStruct((b, s, MLA_ROPE), F32),
                   jax.ShapeDtypeStruct((b, s, DIFF_QKV), F32),
                   jax.ShapeDtypeStruct((b, s, DIFF_QKV), F32)],
        compiler_params=_cparams(("arbitrary", "arbitrary")),
    )(x, mod, g_attn, w_in_p, g_q, w_uq_p, g_kv, cos_q, sin_q, cos_d, sin_d)


def _kv_kernel(lat_ref, kr_ref, dk_ref, dv_ref, wukv_ref, place_ref, km_ref, vm_ref, dkb_ref, dvb_ref):
    kv = jnp.dot(lat_ref[0].astype(BF16), wukv_ref[...], preferred_element_type=F32)
    kv = kv + jnp.dot(kr_ref[0].astype(BF16), place_ref[...], preferred_element_type=F32)
    km_ref[0] = kv[:, :MLA_PAD].astype(BF16)
    vm_ref[0] = kv[:, MLA_PAD:].astype(BF16)
    dkb_ref[0] = dk_ref[0].astype(BF16)
    dvb_ref[0] = dv_ref[0].astype(BF16)


def _kv(lat, kr, dk, dv, w_ukv_p, place, t):
    b, s, _ = lat.shape
    full = lambda shape: pl.BlockSpec(shape, lambda i, j: (0,) * len(shape))
    tok = lambda w: pl.BlockSpec((1, t, w), lambda i, j: (i, j, 0))
    return pl.pallas_call(
        _kv_kernel,
        grid=(b, s // t),
        in_specs=[tok(MLA_KV_LORA), tok(MLA_ROPE), tok(DIFF_QKV), tok(DIFF_QKV),
                  full((MLA_KV_LORA, 2 * MLA_PAD)), full((MLA_ROPE, 2 * MLA_PAD))],
        out_specs=[tok(MLA_PAD), tok(MLA_PAD), tok(DIFF_QKV), tok(DIFF_QKV)],
        out_shape=[jax.ShapeDtypeStruct((b, s, MLA_PAD), BF16), jax.ShapeDtypeStruct((b, s, MLA_PAD), BF16),
                   jax.ShapeDtypeStruct((b, s, DIFF_QKV), BF16), jax.ShapeDtypeStruct((b, s, DIFF_QKV), BF16)],
        compiler_params=_cparams(("arbitrary", "arbitrary")),
    )(lat, kr, dk, dv, w_ukv_p, place)


def _kvt_kernel(lat_ref, kr_ref, dk_ref, dv_ref, wukv_ref, place_ref, wuvt_ref, eye_ref,
                km_ref, vt_ref, dkb_ref, dvt_ref):
    lat = lat_ref[0].astype(BF16)
    k = (jnp.dot(lat, wukv_ref[:, :MLA_PAD], preferred_element_type=F32)
         + jnp.dot(kr_ref[0].astype(BF16), place_ref[:, :MLA_PAD], preferred_element_type=F32))
    km_ref[0] = k.astype(BF16)
    nt = (((1,), (1,)), ((), ()))
    v_t = lax.dot_general(wuvt_ref[...], lat, nt, preferred_element_type=F32)
    for hd in range(MLA_HEADS):
        vt_ref[0, hd, 0] = v_t[hd * LANES:(hd + 1) * LANES, :].astype(BF16)
    dkb_ref[0] = dk_ref[0].astype(BF16)
    dv_t = lax.dot_general(eye_ref[...], dv_ref[0].astype(BF16), nt, preferred_element_type=F32)
    for hd in range(DIFF_HEADS):
        dvt_ref[0, hd, 0] = dv_t[hd * LANES:(hd + 1) * LANES, :].astype(BF16)


def _kvt(lat, kr, dk, dv, w_ukv_p, place, w_uvt_p, t):
    b, s, _ = lat.shape
    nk = s // t
    full = lambda shape: pl.BlockSpec(shape, lambda i, j: (0,) * len(shape))
    tok = lambda w: pl.BlockSpec((1, t, w), lambda i, j: (i, j, 0))
    tr = lambda h: pl.BlockSpec((1, h, 1, LANES, t), lambda i, j: (i, 0, j, 0, 0))
    eye = jnp.eye(DIFF_QKV, dtype=BF16)
    return pl.pallas_call(
        _kvt_kernel,
        grid=(b, nk),
        in_specs=[tok(MLA_KV_LORA), tok(MLA_ROPE), tok(DIFF_QKV), tok(DIFF_QKV),
                  full((MLA_KV_LORA, 2 * MLA_PAD)), full((MLA_ROPE, 2 * MLA_PAD)),
                  full((MLA_PAD, MLA_KV_LORA)), full((DIFF_QKV, DIFF_QKV))],
        out_specs=[tok(MLA_PAD), tr(MLA_HEADS), tok(DIFF_QKV), tr(DIFF_HEADS)],
        out_shape=[jax.ShapeDtypeStruct((b, s, MLA_PAD), BF16),
                   jax.ShapeDtypeStruct((b, MLA_HEADS, nk, LANES, t), BF16),
                   jax.ShapeDtypeStruct((b, s, DIFF_QKV), BF16),
                   jax.ShapeDtypeStruct((b, DIFF_HEADS, nk, LANES, t), BF16)],
        compiler_params=_cparams(("arbitrary", "arbitrary")),
    )(lat, kr, dk, dv, w_ukv_p, place, w_uvt_p, eye)


def _attn_t_kernel(*refs, tq, tk, nk, diff, lam_init):
    if diff:
        (q_ref, k_ref, vt_ref, lq1_ref, lk1_ref, lq2_ref, lk2_ref, gd_ref,
         o_ref, m_ref, l_ref, acc_ref, qs_ref) = refs
    else:
        q_ref, k_ref, vt_ref, o_ref, m_ref, l_ref, acc_ref = refs
    qi = pl.program_id(2)
    cols = 2 * tq if diff else tq

    m_ref[...] = jnp.full(m_ref.shape, NEG_BIG, F32)
    l_ref[...] = jnp.zeros(l_ref.shape, F32)
    acc_ref[...] = jnp.zeros(acc_ref.shape, F32)
    if diff:
        q = q_ref[0]
        lane = lax.broadcasted_iota(I32, q.shape, 1)
        qs_ref[0:tq, :] = jnp.where(lane < DIFF_HD, q, jnp.zeros_like(q))
        qs_ref[tq:, :] = jnp.where(lane >= DIFF_HD, q, jnp.zeros_like(q))

    q_start = qi * tq
    seen_by_all = ((q_start >> CHUNK_SHIFT) + 1) << CHUNK_SHIFT
    seen_by_any = (((q_start + tq - 1) >> CHUNK_SHIFT) + 1) << CHUNK_SHIFT
    n_full = jnp.minimum(seen_by_all // tk, nk)
    n_need = jnp.minimum((seen_by_any + tk - 1) // tk, nk)

    def block(j, masked):
        start = pl.multiple_of(j * tk, tk)
        q = qs_ref[...] if diff else q_ref[0]
        s = lax.dot_general(k_ref[0, pl.ds(start, tk), :], q, (((1,), (1,)), ((), ())),
                            preferred_element_type=F32)
        if masked:
            c = lax.broadcasted_iota(I32, (1, cols), 1)
            if diff:
                c = jnp.where(c >= tq, c - tq, c)
            q_chunk = (q_start + c) >> CHUNK_SHIFT
            k_chunk = (start + lax.broadcasted_iota(I32, (tk, 1), 0)) >> CHUNK_SHIFT
            s = jnp.where(k_chunk <= q_chunk, s, -jnp.inf)
        m_prev = m_ref[...]
        m_new = jnp.maximum(m_prev, jnp.max(s, axis=0, keepdims=True))
        alpha = jnp.exp2(m_prev - m_new)
        p = jnp.exp2(s - m_new)
        l_ref[...] = alpha * l_ref[...] + jnp.sum(p, axis=0, keepdims=True)
        acc_ref[...] = alpha * acc_ref[...] + jnp.dot(vt_ref[0, 0, j], p.astype(BF16), preferred_element_type=F32)
        m_ref[...] = m_new

    def full_body(j, c):
        block(j, False)
        return c

    def masked_body(j, c):
        block(j, True)
        return c

    lax.fori_loop(0, n_full, full_body, 0)
    lax.fori_loop(n_full, n_need, masked_body, 0)

    o_t = acc_ref[...] / l_ref[...]
    if diff:
        lam = (jnp.exp(jnp.sum(lq1_ref[...] * lk1_ref[...], keepdims=True))
               - jnp.exp(jnp.sum(lq2_ref[...] * lk2_ref[...], keepdims=True)) + lam_init)
        od = o_t[:, 0:tq] - lam * o_t[:, tq:]
        od = od * lax.rsqrt(jnp.mean(od * od, axis=0, keepdims=True) + EPS) * gd_ref[...]
        o_ref[0] = (od * (1.0 - lam_init)).T.astype(o_ref.dtype)
    else:
        o_ref[0] = o_t.T.astype(o_ref.dtype)


def _attention_t(q, k, v_t, heads, tq, diff_args=None, lam_init=0.0):
    b, sq, _ = q.shape
    nk, tk = v_t.shape[2], v_t.shape[4]
    diff = diff_args is not None
    in_specs = [pl.BlockSpec((1, tq, LANES), lambda i, h, qi: (i, qi, h)),
                pl.BlockSpec((1, nk * tk, LANES), lambda i, h, qi: (i, 0, h)),
                pl.BlockSpec((1, 1, nk, LANES, tk), lambda i, h, qi: (i, h, 0, 0, 0))]
    cols = 2 * tq if diff else tq
    scratch = [pltpu.VMEM((1, cols), F32), pltpu.VMEM((1, cols), F32), pltpu.VMEM((LANES, cols), F32)]
    args = [q, k, v_t]
    if diff:
        lq1, lk1, lq2, lk2, g_diff = diff_args
        small = pl.BlockSpec((1, DIFF_HD), lambda i, h, qi: (0, 0))
        in_specs += [small] * 4 + [pl.BlockSpec((2 * DIFF_HD, 1), lambda i, h, qi: (0, 0))]
        scratch.append(pltpu.VMEM((cols, LANES), BF16))
        args += [lq1, lk1, lq2, lk2, g_diff.reshape(-1, 1)]
    kern = functools.partial(_attn_t_kernel, tq=tq, tk=tk, nk=nk, diff=diff, lam_init=lam_init)
    return pl.pallas_call(
        kern,
        grid=(b, heads, sq // tq),
        in_specs=in_specs,
        out_specs=pl.BlockSpec((1, tq, LANES), lambda i, h, qi: (i, qi, h)),
        out_shape=jax.ShapeDtypeStruct((b, sq, heads * LANES), BF16),
        scratch_shapes=scratch,
        compiler_params=_cparams(("arbitrary",) * 3),
    )(*args)


def _attn_kernel(*refs, tq, tk, nk, q_off, diff, lam_init):
    if diff:
        (q_ref, k_ref, v_ref, lq1_ref, lk1_ref, lq2_ref, lk2_ref, gd_ref,
         o_ref, m_ref, l_ref, acc_ref, qs_ref) = refs
    else:
        q_ref, k_ref, v_ref, o_ref, m_ref, l_ref, acc_ref = refs
    qi = pl.program_id(2)
    rows = 2 * tq if diff else tq

    m_ref[...] = jnp.full(m_ref.shape, NEG_BIG, F32)
    l_ref[...] = jnp.zeros(l_ref.shape, F32)
    acc_ref[...] = jnp.zeros(acc_ref.shape, F32)
    if diff:
        q = q_ref[0]
        lane = lax.broadcasted_iota(I32, q.shape, 1)
        qs_ref[0:tq, :] = jnp.where(lane < DIFF_HD, q, jnp.zeros_like(q))
        qs_ref[tq:, :] = jnp.where(lane >= DIFF_HD, q, jnp.zeros_like(q))

    q_start = q_off + qi * tq
    seen_by_all = ((q_start >> CHUNK_SHIFT) + 1) << CHUNK_SHIFT
    seen_by_any = (((q_start + tq - 1) >> CHUNK_SHIFT) + 1) << CHUNK_SHIFT
    n_full = jnp.minimum(seen_by_all // tk, nk)
    n_need = jnp.minimum((seen_by_any + tk - 1) // tk, nk)

    def block(j, masked):
        start = pl.multiple_of(j * tk, tk)
        q = qs_ref[...] if diff else q_ref[0]
        s = lax.dot_general(q, k_ref[0, pl.ds(start, tk), :], (((1,), (1,)), ((), ())),
                            preferred_element_type=F32)
        if masked:
            r = lax.broadcasted_iota(I32, (rows, 1), 0)
            if diff:
                r = jnp.where(r >= tq, r - tq, r)
            q_chunk = (q_start + r) >> CHUNK_SHIFT
            k_chunk = (start + lax.broadcasted_iota(I32, (1, tk), 1)) >> CHUNK_SHIFT
            s = jnp.where(k_chunk <= q_chunk, s, -jnp.inf)
        m_prev = m_ref[...]
        m_new = jnp.maximum(m_prev, jnp.max(s, axis=1, keepdims=True))
        alpha = jnp.exp2(m_prev - m_new)
        p = jnp.exp2(s - m_new)
        l_ref[...] = alpha * l_ref[...] + jnp.sum(p, axis=1, keepdims=True)
        acc_ref[...] = alpha * acc_ref[...] + jnp.dot(p.astype(BF16), v_ref[0, pl.ds(start, tk), :],
                                                      preferred_element_type=F32)
        m_ref[...] = m_new

    def full_body(j, c):
        block(j, False)
        return c

    def masked_body(j, c):
        block(j, True)
        return c

    lax.fori_loop(0, n_full, full_body, 0)
    lax.fori_loop(n_full, n_need, masked_body, 0)

    o = acc_ref[...] / l_ref[...]
    if diff:
        lam = (jnp.exp(jnp.sum(lq1_ref[...] * lk1_ref[...], keepdims=True))
               - jnp.exp(jnp.sum(lq2_ref[...] * lk2_ref[...], keepdims=True)) + lam_init)
        od = o[0:tq] - lam * o[tq:]
        o_ref[0] = (_rms(od, gd_ref[...]) * (1.0 - lam_init)).astype(o_ref.dtype)
    else:
        o_ref[0] = o.astype(o_ref.dtype)


def _attention(q, k, v, heads, tq, tk, q_off, diff_args=None, lam_init=0.0):
    b, sq, _ = q.shape
    sk = k.shape[1]
    nq, nk = sq // tq, sk // tk
    diff = diff_args is not None
    kv_spec = pl.BlockSpec((1, sk, LANES), lambda i, h, qi: (i, 0, h))
    in_specs = [pl.BlockSpec((1, tq, LANES), lambda i, h, qi: (i, qi, h)), kv_spec, kv_spec]
    rows = 2 * tq if diff else tq
    scratch = [pltpu.VMEM((rows, 1), F32), pltpu.VMEM((rows, 1), F32), pltpu.VMEM((rows, LANES), F32)]
    args = [q, k, v]
    if diff:
        small = lambda w: pl.BlockSpec((1, w), lambda i, h, qi: (0, 0))
        in_specs += [small(DIFF_HD)] * 4 + [small(2 * DIFF_HD)]
        scratch.append(pltpu.VMEM((rows, LANES), BF16))
        args += list(diff_args)
    kern = functools.partial(_attn_kernel, tq=tq, tk=tk, nk=nk, q_off=q_off, diff=diff, lam_init=lam_init)
    return pl.pallas_call(
        kern,
        grid=(b, heads, nq),
        in_specs=in_specs,
        out_specs=pl.BlockSpec((1, tq, LANES), lambda i, h, qi: (i, qi, h)),
        out_shape=jax.ShapeDtypeStruct((b, sq, heads * LANES), BF16),
        scratch_shapes=scratch,
        compiler_params=_cparams(("arbitrary",) * 3),
    )(*args)


def _out_kernel(x_ref, om_ref, od_ref, wout_ref, mod_ref, gf_ref, wr_ref, br_ref,
                x1_ref, hp_ref, idx_ref, gate_ref):
    a = (jnp.dot(om_ref[0], wout_ref[0:MLA_PAD, :], preferred_element_type=F32)
         + jnp.dot(od_ref[0], wout_ref[MLA_PAD:, :], preferred_element_type=F32))
    mod = mod_ref[0]
    x1 = x_ref[0] + mod[2:3] * a
    x1_ref[0] = x1
    h2 = _rms(x1, gf_ref[...]) * (1.0 + mod[4:5]) + mod[3:4]
    bits = lax.bitcast_convert_type(h2.astype(BF16).astype(F32), U32)
    half = h2.shape[1] // 2
    hp_ref[0] = (bits[:, :half] >> 16) | (bits[:, half:] & jnp.uint32(0xFFFF0000))

    logits = lax.dot_general(wr_ref[...], h2, (((1,), (1,)), ((), ())), precision=lax.Precision.HIGHEST,
                             preferred_element_type=F32) + br_ref[...]
    e_iota = lax.broadcasted_iota(I32, logits.shape, 0).astype(F32)
    vals, idxs = [], []
    for _ in range(TOP_K):
        m = jnp.max(logits, axis=0, keepdims=True)
        i = jnp.min(jnp.where(logits == m, e_iota, float(N_EXPERTS)), axis=0, keepdims=True)
        vals.append(m)
        idxs.append(i)
        logits = jnp.where(e_iota == i, -jnp.inf, logits)
    ex = [jnp.exp(v - vals[0]) for v in vals]
    tot = ex[0] + ex[1] + ex[2] + ex[3]
    idx_ref[0] = jnp.concatenate(idxs, axis=0).astype(I32)
    gate_ref[0] = jnp.concatenate([e / tot for e in ex], axis=0)


def _out(x, om, od, w_out_p, mod, g_ffn, w_r, b_r, t):
    b, s, d = x.shape
    ns = s // t
    full = lambda shape: pl.BlockSpec(shape, lambda i, j: (0,) * len(shape))
    tok = lambda w: pl.BlockSpec((1, t, w), lambda i, j: (i, j, 0))
    flat = pl.BlockSpec((1, TOP_K, t), lambda i, j: (i * ns + j, 0, 0))
    x1, hp, idx, gates = pl.pallas_call(
        _out_kernel,
        grid=(b, ns),
        in_specs=[tok(d), tok(MLA_PAD), tok(DIFF_QKV), full((MLA_PAD + DIFF_QKV, d)),
                  pl.BlockSpec((1, 6, d), lambda i, j: (i, 0, 0)), full((1, d)),
                  full((N_EXPERTS, d)), full((N_EXPERTS, 1))],
        out_specs=[tok(d), tok(d // 2), flat, flat],
        out_shape=[jax.ShapeDtypeStruct((b, s, d), F32), jax.ShapeDtypeStruct((b, s, d // 2), U32),
                   jax.ShapeDtypeStruct((b * ns, TOP_K, t), I32), jax.ShapeDtypeStruct((b * ns, TOP_K, t), F32)],
        compiler_params=_cparams(("arbitrary", "arbitrary")),
    )(x, om, od, w_out_p, mod, g_ffn, w_r, b_r)
    unflat = lambda a: a.transpose(1, 0, 2).reshape(TOP_K, b * s)
    return x1, hp, unflat(idx), unflat(gates)


def _rank_kernel(idx_ref, rank_ref, cnt_ref, carry_ref):
    step = pl.program_id(0)

    @pl.when(step == 0)
    def _():
        carry_ref[...] = jnp.zeros(carry_ref.shape, F32)

    idx = idx_ref[...]
    t = idx.shape[1]
    e_iota = lax.broadcasted_iota(I32, (N_EXPERTS, t), 0)
    hits = [e_iota == idx[k:k + 1, :] for k in range(TOP_K)]
    onehot = sum(h.astype(F32) for h in hits)
    earlier = (lax.broadcasted_iota(I32, (t, t), 0) < lax.broadcasted_iota(I32, (t, t), 1)).astype(BF16)
    before = jnp.dot(onehot.astype(BF16), earlier, preferred_element_type=F32) + carry_ref[...]
    rank_ref[...] = jnp.concatenate(
        [jnp.sum(jnp.where(h, before, 0.0), axis=0, keepdims=True) for h in hits], axis=0).astype(I32)
    carry_ref[...] = carry_ref[...] + jnp.sum(onehot, axis=1, keepdims=True)
    cnt_ref[...] = carry_ref[...].astype(I32)


def _rank(idx, t):
    n = idx.shape[1]
    return pl.pallas_call(
        _rank_kernel,
        grid=(n // t,),
        in_specs=[pl.BlockSpec((TOP_K, t), lambda i: (0, i))],
        out_specs=[pl.BlockSpec((TOP_K, t), lambda i: (0, i)), pl.BlockSpec((N_EXPERTS, 1), lambda i: (0, 0))],
        out_shape=[jax.ShapeDtypeStruct((TOP_K, n), I32), jax.ShapeDtypeStruct((N_EXPERTS, 1), I32)],
        scratch_shapes=[pltpu.VMEM((N_EXPERTS, 1), F32)],
        compiler_params=_cparams(("arbitrary",)),
    )(idx)


def _dispatch_kernel(dest_ref, hp_ref, xs_in_ref, xs_ref, dest_smem, sem, idx_sem, *, t):
    del xs_in_ref
    cp = pltpu.make_async_copy(dest_ref.at[0], dest_smem, idx_sem)
    cp.start()
    cp.wait()

    def row_copy(tok, k):
        return pltpu.make_async_copy(hp_ref.at[pl.ds(tok, 1)], xs_ref.at[pl.ds(dest_smem[k, tok], 1)], sem)

    def issue(tok, c):
        for k in range(TOP_K):
            row_copy(tok, k).start()
        return c

    lax.fori_loop(0, t, issue, 0, unroll=ISSUE_UNROLL)
    for k in range(TOP_K):
        pltpu.make_async_copy(hp_ref, xs_ref.at[pl.ds(0, t)], sem).wait()


def _dispatch(dest3, hp, n_slots, t):
    n, w = hp.shape
    zeros = jnp.zeros((n_slots, w), U32)
    return pl.pallas_call(
        functools.partial(_dispatch_kernel, t=t),
        grid=(n // t,),
        in_specs=[pl.BlockSpec((1, TOP_K, t), lambda i: (i, 0, 0)),
                  pl.BlockSpec((t, w), lambda i: (i, 0)),
                  pl.BlockSpec(memory_space=pl.ANY)],
        out_specs=pl.BlockSpec(memory_space=pl.ANY),
        out_shape=jax.ShapeDtypeStruct((n_slots, w), U32),
        scratch_shapes=[pltpu.SMEM((TOP_K, t), I32), pltpu.SemaphoreType.DMA, pltpu.SemaphoreType.DMA],
        input_output_aliases={2: 0},
        compiler_params=_cparams(("arbitrary",)),
    )(dest3, hp, zeros)


def _expert_kernel(te_ref, nu_ref, xs_ref, wgu_ref, bgu_ref, wd_ref, bd_ref, ys_ref):
    del te_ref

    @pl.when(pl.program_id(0) < nu_ref[0])
    def _():
        p = xs_ref[...]
        x_lo = lax.bitcast_convert_type(p << 16, F32).astype(BF16)
        x_hi = lax.bitcast_convert_type(p & jnp.uint32(0xFFFF0000), F32).astype(BF16)
        half = p.shape[1]
        gu = (jnp.dot(x_lo, wgu_ref[0, 0:half, :], preferred_element_type=F32)
              + jnp.dot(x_hi, wgu_ref[0, half:, :], preferred_element_type=F32) + bgu_ref[0])
        f = gu.shape[1] // 2
        glu = jnp.minimum(gu[:, :f], SWIGLU_LIMIT)
        lin = jnp.clip(gu[:, f:], -SWIGLU_LIMIT, SWIGLU_LIMIT)
        act = glu * jax.nn.sigmoid(SWIGLU_ALPHA * glu) * (lin + 1.0)
        ys_ref[...] = jnp.dot(act.astype(BF16), wd_ref[0], preferred_element_type=F32) + bd_ref[0]


def _experts(tile_e, n_used, xs, w_gu, b_gu, w_down, b_down):
    n_slots, half = xs.shape
    e, d, f2 = w_gu.shape
    n_tiles = n_slots // MOE_TILE
    row = lambda i, te, nu: (jnp.minimum(i, nu[0] - 1), 0)
    wsel = lambda i, te, nu: (te[i], 0, 0)
    grid_spec = pltpu.PrefetchScalarGridSpec(
        num_scalar_prefetch=2,
        grid=(n_tiles,),
        in_specs=[pl.BlockSpec((MOE_TILE, half), row),
                  pl.BlockSpec((1, d, f2), wsel), pl.BlockSpec((1, 1, f2), wsel),
                  pl.BlockSpec((1, f2 // 2, d), wsel), pl.BlockSpec((1, 1, d), wsel)],
        out_specs=pl.BlockSpec((MOE_TILE, d), row),
    )
    return pl.pallas_call(
        _expert_kernel,
        grid_spec=grid_spec,
        out_shape=jax.ShapeDtypeStruct((n_slots, d), F32),
        compiler_params=_cparams(("arbitrary",)),
    )(tile_e, n_used, xs, w_gu, b_gu.reshape(e, 1, f2), w_down, b_down.reshape(e, 1, d))


def _combine_kernel(dcur_ref, dnxt_ref, gate_ref, x1_ref, mod_ref, gfin_ref, ys_ref,
                    x2_ref, buf, dsm, sem, idx_sem, *, t, n_steps, final):
    i = pl.program_id(0)
    slot = i % 2

    def load_idx(src, s):
        cp = pltpu.make_async_copy(src.at[0], dsm.at[s], idx_sem)
        cp.start()
        cp.wait()

    def row_copy(s, tok, k):
        return pltpu.make_async_copy(ys_ref.at[pl.ds(dsm[s, k, tok], 1)], buf.at[s, k, pl.ds(tok, 1)], sem.at[s])

    def issue(s):
        def body(tok, c):
            for k in range(TOP_K):
                row_copy(s, tok, k).start()
            return c
        lax.fori_loop(0, t, body, 0, unroll=ISSUE_UNROLL)

    @pl.when(i == 0)
    def _():
        load_idx(dcur_ref, 0)
        issue(0)

    @pl.when(i + 1 < n_steps)
    def _():
        load_idx(dnxt_ref, 1 - slot)
        issue(1 - slot)

    for k in range(TOP_K):
        pltpu.make_async_copy(ys_ref.at[pl.ds(0, t)], buf.at[slot, k], sem.at[slot]).wait()

    g = gate_ref[...]
    y = g[:, 0:1] * buf[slot, 0]
    for k in range(1, TOP_K):
        y = y + g[:, k:k + 1] * buf[slot, k]
    x2 = x1_ref[...] + mod_ref[0][5:6] * y
    x2_ref[...] = _rms(x2, gfin_ref[...]) if final else x2


def _combine(dest3, gates_t, x1, mod, g_final, ys, s, t, final):
    n, d = x1.shape
    n_steps = n // t
    per_b = s // t
    return pl.pallas_call(
        functools.partial(_combine_kernel, t=t, n_steps=n_steps, final=final),
        grid=(n_steps,),
        in_specs=[pl.BlockSpec((1, TOP_K, t), lambda i: (i, 0, 0)),
                  pl.BlockSpec((1, TOP_K, t), lambda i: (jnp.minimum(i + 1, n_steps - 1), 0, 0)),
                  pl.BlockSpec((t, TOP_K), lambda i: (i, 0)),
                  pl.BlockSpec((t, d), lambda i: (i, 0)),
                  pl.BlockSpec((1, 6, d), lambda i: (i // per_b, 0, 0)),
                  pl.BlockSpec((1, d), lambda i: (0, 0)),
                  pl.BlockSpec(memory_space=pl.ANY)],
        out_specs=pl.BlockSpec((t, d), lambda i: (i, 0)),
        out_shape=jax.ShapeDtypeStruct((n, d), F32),
        scratch_shapes=[pltpu.VMEM((2, TOP_K, t, d), F32), pltpu.SMEM((2, TOP_K, t), I32),
                        pltpu.SemaphoreType.DMA((2,)), pltpu.SemaphoreType.DMA],
        compiler_params=_cparams(("arbitrary",)),
    )(dest3, dest3, gates_t, x1, mod, g_final, ys)


def _rope_tables(pos, d, lane_pos):
    inv = ROPE_THETA ** (-jnp.arange(0, d, 2, dtype=F32) / d)
    ang = pos.astype(F32)[:, None] * inv[None, :]
    cos, sin = jnp.cos(ang), jnp.sin(ang)
    cos_t = jnp.ones((pos.shape[0], LANES), F32)
    sin_t = jnp.zeros((pos.shape[0], LANES), F32)
    for o in lane_pos:
        cos_t = cos_t.at[:, o:o + d].set(jnp.concatenate([cos, cos], axis=1))
        sin_t = sin_t.at[:, o:o + d].set(jnp.concatenate([-sin, sin], axis=1))
    return cos_t, sin_t


def _prep_weights(w_in, w_uq, w_uk, w_uv, w_out):
    depth, d, _ = w_in.shape
    cuts = [MLA_Q_LORA, MLA_KV_LORA, MLA_ROPE, DIFF_QKV, DIFF_QKV, DIFF_QKV]
    offs = [0]
    for c in cuts:
        offs.append(offs[-1] + c)
    parts = [w_in[:, :, offs[i]:offs[i + 1]] for i in range(6)]
    kr_pad = jnp.pad(parts[2], ((0, 0), (0, 0), (MLA_NOPE, LANES - MLA_NOPE - MLA_ROPE)))
    w_in_p = jnp.concatenate([parts[0], parts[1], kr_pad, parts[3], parts[4], parts[5]], axis=2).astype(BF16)
    qd = MLA_NOPE + MLA_ROPE
    w_uq_p = jnp.pad(w_uq.reshape(depth, MLA_Q_LORA, MLA_HEADS, qd), ((0, 0), (0, 0), (0, 0), (0, LANES - qd)))
    w_uq_p = w_uq_p.reshape(depth, MLA_Q_LORA, MLA_PAD).astype(BF16)
    pad_head = lambda w, n: jnp.pad(w, ((0, 0), (0, 0), (0, 0), (0, LANES - n))).reshape(depth, MLA_KV_LORA, MLA_PAD)
    w_ukv_p = jnp.concatenate([pad_head(w_uk, MLA_NOPE), pad_head(w_uv, MLA_V)], axis=2).astype(BF16)
    wo_m = jnp.pad(w_out[:, :MLA_HEADS * MLA_V].reshape(depth, MLA_HEADS, MLA_V, d),
                   ((0, 0), (0, 0), (0, LANES - MLA_V), (0, 0))).reshape(depth, MLA_PAD, d)
    w_out_p = jnp.concatenate([wo_m, w_out[:, MLA_HEADS * MLA_V:]], axis=1).astype(BF16)
    place = jnp.zeros((MLA_ROPE, 2 * MLA_PAD), F32)
    eye = jnp.eye(MLA_ROPE, dtype=F32)
    for hd in range(MLA_HEADS):
        place = place.at[:, hd * LANES + MLA_NOPE:hd * LANES + MLA_NOPE + MLA_ROPE].set(eye)
    w_uvt_p = jnp.swapaxes(w_ukv_p[:, :, MLA_PAD:], 1, 2)
    return w_in_p, w_uq_p, w_ukv_p, w_out_p, place.astype(BF16), w_uvt_p


def _pick(n, pref):
    t = min(n, pref)
    while n % t:
        t //= 2
    return t


def _moe(hp, idx, gates, x1, mod_l, g_final, wts, s, final):
    w_gu, b_gu, w_down, b_down = wts
    n = hp.shape[0]
    rank, counts = _rank(idx, _pick(n, RANK_TILE))
    counts = counts[:, 0]
    padded = (counts + MOE_TILE - 1) // MOE_TILE * MOE_TILE
    pad_ends = jnp.cumsum(padded)
    pad_starts = pad_ends - padded
    dest = rank + jnp.sum(jnp.where(idx[None] == jnp.arange(N_EXPERTS, dtype=I32)[:, None, None],
                                    pad_starts[:, None, None], 0), axis=0)
    n_tiles = (n * TOP_K) // MOE_TILE + N_EXPERTS
    tile_e = jnp.minimum(jnp.sum(pad_ends[None, :] <= (jnp.arange(n_tiles, dtype=I32) * MOE_TILE)[:, None], axis=1),
                         N_EXPERTS - 1).astype(I32)
    n_used = (pad_ends[-1:] // MOE_TILE).astype(I32)
    td = _pick(n, DISPATCH_TILE)
    xs = _dispatch(dest.reshape(TOP_K, n // td, td).transpose(1, 0, 2), hp, n_tiles * MOE_TILE, td)
    ys = _experts(tile_e, n_used, xs, w_gu, b_gu, w_down, b_down)
    tc = _pick(s, COMBINE_TILE)
    return _combine(dest.reshape(TOP_K, n // tc, tc).transpose(1, 0, 2), gates.T, x1, mod_l, g_final, ys, s, tc,
                    final)


def _trunk(x, c, caches, prm, prepped, q_off):
    b, s, d = x.shape
    depth = prm['w_in'].shape[0]
    w_in_p, w_uq_p, w_ukv_p, w_out_p, place, w_uvt_p, w_ada_b, w_gu_b, w_down_b = prepped
    rows = -(-b // 16) * 16
    mod = _ada(jnp.pad(c, ((0, rows - b), (0, 0))), w_ada_b, prm['b_ada'])[:, :b].reshape(depth, b, 6, d)
    pos = q_off + jnp.arange(s, dtype=I32)
    cos_q, sin_q = _rope_tables(pos, MLA_ROPE, (MLA_NOPE,))
    cos_d, sin_d = _rope_tables(pos, DIFF_HD, (0, DIFF_HD))
    t = _pick(s, TOK_TILE)
    new_rows = ([], [], [], [])
    row = lambda v: v.reshape(1, -1)
    for l in range(depth):
        qm, qd, lat, kr, dk, dv = _proj(x, mod[l], row(prm['g_attn'][l]), w_in_p[l], row(prm['g_q'][l]), w_uq_p[l],
                                        row(prm['g_kv'][l]), (cos_q, sin_q, cos_d, sin_d), t)
        for lst, r in zip(new_rows, (lat, kr, dk, dv)):
            lst.append(r)
        if caches is not None:
            lat_a, kr_a, dk_a, dv_a = (jnp.concatenate([cc[l].reshape(b, cc.shape[2], -1), r], axis=1)
                                       for cc, r in zip(caches, (lat, kr, dk, dv)))
        else:
            lat_a, kr_a, dk_a, dv_a = lat, kr, dk, dv
        sk = lat_a.shape[1]
        lam_init = 0.8 - 0.6 * math.exp(-0.3 * l)
        diff_args = (row(prm['lam_q1'][l]), row(prm['lam_k1'][l]), row(prm['lam_q2'][l]),
                     row(prm['lam_k2'][l]), row(prm['g_diff'][l]))
        if caches is None and q_off == 0 and sk % ATTN_TK == 0:
            km, vm_t, dkb, dv_t = _kvt(lat_a, kr_a, dk_a, dv_a, w_ukv_p[l], place, w_uvt_p[l], ATTN_TK)
            om = _attention_t(qm, km, vm_t, MLA_HEADS, _pick(s, ATTN_TQ))
            od = _attention_t(qd, dkb, dv_t, DIFF_HEADS, _pick(s, ATTN_TQ // 2), diff_args, lam_init)
        else:
            tk = ATTN_TK if sk % ATTN_TK == 0 else sk
            tkv = max(t_ for t_ in range(16, min(sk, 2 * TOK_TILE) + 1, 16) if sk % t_ == 0)
            km, vm, dkb, dvb = _kv(lat_a, kr_a, dk_a, dv_a, w_ukv_p[l], place, tkv)
            om = _attention(qm, km, vm, MLA_HEADS, _pick(s, ATTN_TQ), tk, q_off)
            od = _attention(qd, dkb, dvb, DIFF_HEADS, _pick(s, ATTN_TQ // 2), tk, q_off, diff_args, lam_init)
        x1, hp, idx, gates = _out(x, om, od, w_out_p[l], mod[l], row(prm['g_ffn'][l]),
                                  prm['w_router'][l].T, prm['b_router'][l].reshape(-1, 1), t)
        x = _moe(hp.reshape(b * s, d // 2), idx, gates, x1.reshape(b * s, d), mod[l], row(prm['g_final']),
                 (w_gu_b[l], prm['b_gate_up'][l], w_down_b[l], prm['b_down'][l]), s, l == depth - 1).reshape(b, s, d)
    lat_n, kr_n, dk_n, dv_n = (jnp.stack(lst) for lst in new_rows)
    shp = (depth, b, s, DIFF_HEADS, 2 * DIFF_HD)
    return x, (lat_n, kr_n, dk_n.reshape(shp), dv_n.reshape(shp))


def kernel(x_prompt, x_sample, c_prompt, c_sample, cache_mla_latent, cache_mla_krope, cache_diff_k, cache_diff_v,
           w_ada, b_ada, g_attn, g_ffn, w_in, g_q, w_uq, g_kv, w_uk, w_uv, lam_q1, lam_k1, lam_q2, lam_k2, g_diff,
           w_out, w_router, b_router, w_gate_up, b_gate_up, w_down, b_down, g_final):
    prm = {'b_ada': b_ada, 'g_attn': g_attn, 'g_ffn': g_ffn, 'w_in': w_in, 'g_q': g_q, 'g_kv': g_kv,
           'lam_q1': lam_q1, 'lam_k1': lam_k1, 'lam_q2': lam_q2, 'lam_k2': lam_k2, 'g_diff': g_diff,
           'w_router': w_router, 'b_router': b_router, 'b_gate_up': b_gate_up, 'b_down': b_down,
           'g_final': g_final}
    prepped = _prep_weights(w_in, w_uq, w_uk, w_uv, w_out) + (
        w_ada.astype(BF16), w_gate_up.astype(BF16), w_down.astype(BF16))
    y_p, rows_p = _trunk(x_prompt, c_prompt, None, prm, prepped, 0)
    caches = (cache_mla_latent, cache_mla_krope, cache_diff_k, cache_diff_v)
    y_s, rows_s = _trunk(x_sample, c_sample, caches, prm, prepped, cache_mla_latent.shape[2])
    return (y_p, y_s) + rows_p + rows_s
```

```python
import functools
import math

import jax
import jax.numpy as jnp
from jax import lax
from jax.experimental import pallas as pl
from jax.experimental.pallas import tpu as pltpu

F32 = jnp.float32
BF16 = jnp.bfloat16
I32 = jnp.int32
U32 = jnp.uint32

CHUNK = 64
ROPE_THETA = 10000.0
EPS = 1e-6
MLA_HEADS = 8
MLA_Q_LORA = 256
MLA_KV_LORA = 128
MLA_NOPE = 64
MLA_ROPE = 32
MLA_V = 64
DIFF_HEADS = 4
DIFF_HD = 64
DIFF_QKV = DIFF_HEADS * 2 * DIFF_HD
N_EXPERTS = 32
TOP_K = 4
SWIGLU_LIMIT = 7.0
SWIGLU_ALPHA = 1.702

LANES = 128
MLA_PAD = MLA_HEADS * LANES
PROJ_COLS = MLA_Q_LORA + MLA_KV_LORA + LANES + 3 * DIFF_QKV
VMEM_LIMIT = 56 * 1024 * 1024
MOE_TILE = 256
TOK_TILE = 512
ATTN_TQ = 1024
ATTN_TK = 512
RANK_TILE = 512
DISPATCH_TILE = 256
COMBINE_TILE = 128
ISSUE_UNROLL = 4
NEG_BIG = -1e30
LOG2E = math.log2(math.e)
assert CHUNK & (CHUNK - 1) == 0
CHUNK_SHIFT = CHUNK.bit_length() - 1


def _cparams(sem):
    return pltpu.CompilerParams(dimension_semantics=sem, vmem_limit_bytes=VMEM_LIMIT)


def _rms(x, g):
    return x * lax.rsqrt(jnp.mean(x * x, axis=-1, keepdims=True) + EPS) * g


def _rope_lanes(v, cos, sin_signed, half):
    lane = lax.broadcasted_iota(I32, v.shape, 1)
    first = (lane % (2 * half)) < half
    swapped = jnp.where(first, pltpu.roll(v, LANES - half, 1), pltpu.roll(v, half, 1))
    return v * cos + swapped * sin_signed


def _ada_kernel(c_ref, w_ref, b_ref, o_ref):
    c = c_ref[...]
    s = c * jax.nn.sigmoid(c)
    o_ref[0] = jnp.dot(s.astype(BF16), w_ref[0], preferred_element_type=F32) + b_ref[0]


def _ada(c, w_ada, b_ada):
    depth, d, n6 = w_ada.shape
    rows = c.shape[0]
    tn = 1536
    return pl.pallas_call(
        _ada_kernel,
        grid=(depth, n6 // tn),
        in_specs=[pl.BlockSpec((rows, d), lambda l, j: (0, 0)),
                  pl.BlockSpec((1, d, tn), lambda l, j: (l, 0, j)),
                  pl.BlockSpec((1, 1, tn), lambda l, j: (l, 0, j))],
        out_specs=pl.BlockSpec((1, rows, tn), lambda l, j: (l, 0, j)),
        out_shape=jax.ShapeDtypeStruct((depth, rows, n6), F32),
        compiler_params=_cparams(("arbitrary", "arbitrary")),
    )(c, w_ada, b_ada.reshape(depth, 1, n6))


def _proj_kernel(x_ref, mod_ref, ga_ref, win_ref, gq_ref, wuq_ref, gkv_ref,
                 cq_ref, sq_ref, cd_ref, sd_ref,
                 qm_ref, qd_ref, lat_ref, kr_ref, dk_ref, dv_ref):
    x = x_ref[0]
    mod = mod_ref[0]
    h = _rms(x, ga_ref[...]) * (1.0 + mod[1:2]) + mod[0:1]
    proj = jnp.dot(h.astype(BF16), win_ref[...], preferred_element_type=F32)
    o = 0
    cq = proj[:, o:o + MLA_Q_LORA]; o += MLA_Q_LORA
    ckv = proj[:, o:o + MLA_KV_LORA]; o += MLA_KV_LORA
    krp = proj[:, o:o + LANES]; o += LANES
    dq = proj[:, o:o + DIFF_QKV]; o += DIFF_QKV
    dk = proj[:, o:o + DIFF_QKV]; o += DIFF_QKV
    dv = proj[:, o:o + DIFF_QKV]

    cos_q, sin_q = cq_ref[...], sq_ref[...]
    cos_d, sin_d = cd_ref[...], sd_ref[...]
    q = jnp.dot(_rms(cq, gq_ref[...]).astype(BF16), wuq_ref[...], preferred_element_type=F32)
    q_scale = LOG2E / math.sqrt(MLA_NOPE + MLA_ROPE)
    for hd in range(MLA_HEADS):
        sl = slice(hd * LANES, (hd + 1) * LANES)
        qm_ref[0, :, sl] = (_rope_lanes(q[:, sl], cos_q, sin_q, MLA_ROPE // 2) * q_scale).astype(BF16)
    lat_ref[0] = _rms(ckv, gkv_ref[...])
    kr_ref[0] = _rope_lanes(krp, cos_q, sin_q, MLA_ROPE // 2)[:, MLA_NOPE:MLA_NOPE + MLA_ROPE]
    d_scale = LOG2E / math.sqrt(DIFF_HD)
    for hd in range(DIFF_HEADS):
        sl = slice(hd * LANES, (hd + 1) * LANES)
        qd_ref[0, :, sl] = (_rope_lanes(dq[:, sl], cos_d, sin_d, DIFF_HD // 2) * d_scale).astype(BF16)
        dk_ref[0, :, sl] = _rope_lanes(dk[:, sl], cos_d, sin_d, DIFF_HD // 2)
    dv_ref[0] = dv


def _proj(x, mod, g_attn, w_in_p, g_q, w_uq_p, g_kv, tabs, t):
    b, s, d = x.shape
    cos_q, sin_q, cos_d, sin_d = tabs
    full = lambda shape: pl.BlockSpec(shape, lambda i, j: (0,) * len(shape))
    tok = lambda w: pl.BlockSpec((1, t, w), lambda i, j: (i, j, 0))
    tab = pl.BlockSpec((t, LANES), lambda i, j: (j, 0))
    return pl.pallas_call(
        _proj_kernel,
        grid=(b, s // t),
        in_specs=[tok(d), pl.BlockSpec((1, 6, d), lambda i, j: (i, 0, 0)), full((1, d)),
                  full((d, PROJ_COLS)), full((1, MLA_Q_LORA)), full((MLA_Q_LORA, MLA_PAD)),
                  full((1, MLA_KV_LORA)), tab, tab, tab, tab],
        out_specs=[tok(MLA_PAD), tok(DIFF_QKV), tok(MLA_KV_LORA), tok(MLA_ROPE), tok(DIFF_QKV), tok(DIFF_QKV)],
        out_shape=[jax.ShapeDtypeStruct((b, s, MLA_PAD), BF16),
                   jax.ShapeDtypeStruct((b, s, DIFF_QKV), BF16),
                   jax.ShapeDtypeStruct((b, s, MLA_KV_LORA), F32),
                   jax.ShapeDtypeStruct((b, s, MLA_ROPE), F32),
                   jax.ShapeDtypeStruct((b, s, DIFF_QKV), F32),
                   jax.ShapeDtypeStruct((b, s, DIFF_QKV), F32)],
        compiler_params=_cparams(("arbitrary", "arbitrary")),
    )(x, mod, g_attn, w_in_p, g_q, w_uq_p, g_kv, cos_q, sin_q, cos_d, sin_d)


def _kv_kernel(lat_ref, kr_ref, dk_ref, dv_ref, wukv_ref, place_ref, km_ref, vm_ref, dkb_ref, dvb_ref):
    kv = jnp.dot(lat_ref[0].astype(BF16), wukv_ref[...], preferred_element_type=F32)
    kv = kv + jnp.dot(kr_ref[0].astype(BF16), place_ref[...], preferred_element_type=F32)
    km_ref[0] = kv[:, :MLA_PAD].astype(BF16)
    vm_ref[0] = kv[:, MLA_PAD:].astype(BF16)
    dkb_ref[0] = dk_ref[0].astype(BF16)
    dvb_ref[0] = dv_ref[0].astype(BF16)


def _kv(lat, kr, dk, dv, w_ukv_p, place, t):
    b, s, _ = lat.shape
    full = lambda shape: pl.BlockSpec(shape, lambda i, j: (0,) * len(shape))
    tok = lambda w: pl.BlockSpec((1, t, w), lambda i, j: (i, j, 0))
    return pl.pallas_call(
        _kv_kernel,
        grid=(b, s // t),
        in_specs=[tok(MLA_KV_LORA), tok(MLA_ROPE), tok(DIFF_QKV), tok(DIFF_QKV),
                  full((MLA_KV_LORA, 2 * MLA_PAD)), full((MLA_ROPE, 2 * MLA_PAD))],
        out_specs=[tok(MLA_PAD), tok(MLA_PAD), tok(DIFF_QKV), tok(DIFF_QKV)],
        out_shape=[jax.ShapeDtypeStruct((b, s, MLA_PAD), BF16), jax.ShapeDtypeStruct((b, s, MLA_PAD), BF16),
                   jax.ShapeDtypeStruct((b, s, DIFF_QKV), BF16), jax.ShapeDtypeStruct((b, s, DIFF_QKV), BF16)],
        compiler_params=_cparams(("arbitrary", "arbitrary")),
    )(lat, kr, dk, dv, w_ukv_p, place)


def _kvt_kernel(lat_ref, kr_ref, dk_ref, dv_ref, wukv_ref, place_ref, wuvt_ref, eye_ref,
                km_ref, vt_ref, dkb_ref, dvt_ref):
    lat = lat_ref[0].astype(BF16)
    k = (jnp.dot(lat, wukv_ref[:, :MLA_PAD], preferred_element_type=F32)
         + jnp.dot(kr_ref[0].astype(BF16), place_ref[:, :MLA_PAD], preferred_element_type=F32))
    km_ref[0] = k.astype(BF16)
    nt = (((1,), (1,)), ((), ()))
    v_t = lax.dot_general(wuvt_ref[...], lat, nt, preferred_element_type=F32)
    for hd in range(MLA_HEADS):
        vt_ref[0, hd, 0] = v_t[hd * LANES:(hd + 1) * LANES, :].astype(BF16)
    dkb_ref[0] = dk_ref[0].astype(BF16)
    dv_t = lax.dot_general(eye_ref[...], dv_ref[0].astype(BF16), nt, preferred_element_type=F32)
    for hd in range(DIFF_HEADS):
        dvt_ref[0, hd, 0] = dv_t[hd * LANES:(hd + 1) * LANES, :].astype(BF16)


def _kvt(lat, kr, dk, dv, w_ukv_p, place, w_uvt_p, t):
    b, s, _ = lat.shape
    nk = s // t
    full = lambda shape: pl.BlockSpec(shape, lambda i, j: (0,) * len(shape))
    tok = lambda w: pl.BlockSpec((1, t, w), lambda i, j: (i, j, 0))
    tr = lambda h: pl.BlockSpec((1, h, 1, LANES, t), lambda i, j: (i, 0, j, 0, 0))
    eye = jnp.eye(DIFF_QKV, dtype=BF16)
    return pl.pallas_call(
        _kvt_kernel,
        grid=(b, nk),
        in_specs=[tok(MLA_KV_LORA), tok(MLA_ROPE), tok(DIFF_QKV), tok(DIFF_QKV),
                  full((MLA_KV_LORA, 2 * MLA_PAD)), full((MLA_ROPE, 2 * MLA_PAD)),
                  full((MLA_PAD, MLA_KV_LORA)), full((DIFF_QKV, DIFF_QKV))],
        out_specs=[tok(MLA_PAD), tr(MLA_HEADS), tok(DIFF_QKV), tr(DIFF_HEADS)],
        out_shape=[jax.ShapeDtypeStruct((b, s, MLA_PAD), BF16),
                   jax.ShapeDtypeStruct((b, MLA_HEADS, nk, LANES, t), BF16),
                   jax.ShapeDtypeStruct((b, s, DIFF_QKV), BF16),
                   jax.ShapeDtypeStruct((b, DIFF_HEADS, nk, LANES, t), BF16)],
        compiler_params=_cparams(("arbitrary", "arbitrary")),
    )(lat, kr, dk, dv, w_ukv_p, place, w_uvt_p, eye)


def _attn_t_kernel(*refs, tq, tk, nk, diff, lam_init):
    if diff:
        (q_ref, k_ref, vt_ref, lq1_ref, lk1_ref, lq2_ref, lk2_ref, gd_ref,
         o_ref, m_ref, l_ref, acc_ref, sa_ref, sb_ref, qs_ref) = refs
    else:
        q_ref, k_ref, vt_ref, o_ref, m_ref, l_ref, acc_ref, sa_ref, sb_ref = refs
    qi = pl.program_id(2)
    cols = 2 * tq if diff else tq

    m_ref[...] = jnp.full(m_ref.shape, NEG_BIG, F32)
    l_ref[...] = jnp.zeros(l_ref.shape, F32)
    acc_ref[...] = jnp.zeros(acc_ref.shape, F32)
    if diff:
        q = q_ref[0]
        lane = lax.broadcasted_iota(I32, q.shape, 1)
        qs_ref[0:tq, :] = jnp.where(lane < DIFF_HD, q, jnp.zeros_like(q))
        qs_ref[tq:, :] = jnp.where(lane >= DIFF_HD, q, jnp.zeros_like(q))

    q_start = qi * tq
    seen_by_all = ((q_start >> CHUNK_SHIFT) + 1) << CHUNK_SHIFT
    seen_by_any = (((q_start + tq - 1) >> CHUNK_SHIFT) + 1) << CHUNK_SHIFT
    n_full = jnp.minimum(seen_by_all // tk, nk)
    n_need = jnp.minimum((seen_by_any + tk - 1) // tk, nk)

    def scores(j):
        start = pl.multiple_of(jnp.minimum(j, nk - 1) * tk, tk)
        q = qs_ref[...] if diff else q_ref[0]
        return lax.dot_general(k_ref[0, pl.ds(start, tk), :], q, (((1,), (1,)), ((), ())),
                               preferred_element_type=F32)

    def absorb(s_ref, j, masked):
        s = s_ref[...]
        if masked:
            c = lax.broadcasted_iota(I32, (1, cols), 1)
            if diff:
                c = jnp.where(c >= tq, c - tq, c)
            q_chunk = (q_start + c) >> CHUNK_SHIFT
            k_chunk = (j * tk + lax.broadcasted_iota(I32, (tk, 1), 0)) >> CHUNK_SHIFT
            s = jnp.where(k_chunk <= q_chunk, s, -jnp.inf)
        m_prev = m_ref[...]
        m_new = jnp.maximum(m_prev, jnp.max(s, axis=0, keepdims=True))
        alpha = jnp.exp2(m_prev - m_new)
        p = jnp.exp2(s - m_new)
        l_ref[...] = alpha * l_ref[...] + jnp.sum(p, axis=0, keepdims=True)
        acc_ref[...] = alpha * acc_ref[...] + jnp.dot(vt_ref[0, 0, j], p.astype(BF16), preferred_element_type=F32)
        m_ref[...] = m_new

    sa_ref[...] = scores(0)

    def pair(i, carry):
        j = 2 * i
        sb_ref[...] = scores(j + 1)
        absorb(sa_ref, j, False)
        sa_ref[...] = scores(j + 2)
        absorb(sb_ref, j + 1, False)
        return carry

    n_pairs = n_full // 2
    lax.fori_loop(0, n_pairs, pair, 0)
    done = 2 * n_pairs
    left = n_need - done

    @pl.when(left >= 1)
    def _():
        sb_ref[...] = scores(done + 1)
        absorb(sa_ref, done, True)

    @pl.when(left >= 2)
    def _():
        absorb(sb_ref, done + 1, True)

    def rest(j, carry):
        sa_ref[...] = scores(j)
        absorb(sa_ref, j, True)
        return carry

    lax.fori_loop(done + 2, n_need, rest, 0)

    o_t = acc_ref[...] / l_ref[...]
    if diff:
        lam = (jnp.exp(jnp.sum(lq1_ref[...] * lk1_ref[...], keepdims=True))
               - jnp.exp(jnp.sum(lq2_ref[...] * lk2_ref[...], keepdims=True)) + lam_init)
        od = o_t[:, 0:tq] - lam * o_t[:, tq:]
        od = od * lax.rsqrt(jnp.mean(od * od, axis=0, keepdims=True) + EPS) * gd_ref[...]
        o_ref[0] = (od * (1.0 - lam_init)).T.astype(o_ref.dtype)
    else:
        o_ref[0] = o_t.T.astype(o_ref.dtype)


def _attention_t(q, k, v_t, heads, tq, diff_args=None, lam_init=0.0):
    b, sq, _ = q.shape
    nk, tk = v_t.shape[2], v_t.shape[4]
    diff = diff_args is not None
    in_specs = [pl.BlockSpec((1, tq, LANES), lambda i, h, qi: (i, qi, h)),
                pl.BlockSpec((1, nk * tk, LANES), lambda i, h, qi: (i, 0, h)),
                pl.BlockSpec((1, 1, nk, LANES, tk), lambda i, h, qi: (i, h, 0, 0, 0))]
    cols = 2 * tq if diff else tq
    scratch = [pltpu.VMEM((1, cols), F32), pltpu.VMEM((1, cols), F32), pltpu.VMEM((LANES, cols), F32),
               pltpu.VMEM((tk, cols), F32), pltpu.VMEM((tk, cols), F32)]
    args = [q, k, v_t]
    if diff:
        lq1, lk1, lq2, lk2, g_diff = diff_args
        small = pl.BlockSpec((1, DIFF_HD), lambda i, h, qi: (0, 0))
        in_specs += [small] * 4 + [pl.BlockSpec((2 * DIFF_HD, 1), lambda i, h, qi: (0, 0))]
        scratch.append(pltpu.VMEM((cols, LANES), BF16))
        args += [lq1, lk1, lq2, lk2, g_diff.reshape(-1, 1)]
    kern = functools.partial(_attn_t_kernel, tq=tq, tk=tk, nk=nk, diff=diff, lam_init=lam_init)
    return pl.pallas_call(
        kern,
        grid=(b, heads, sq // tq),
        in_specs=in_specs,
        out_specs=pl.BlockSpec((1, tq, LANES), lambda i, h, qi: (i, qi, h)),
        out_shape=jax.ShapeDtypeStruct((b, sq, heads * LANES), BF16),
        scratch_shapes=scratch,
        compiler_params=_cparams(("arbitrary",) * 3),
    )(*args)


def _attn_kernel(*refs, tq, tk, nk, q_off, diff, lam_init):
    if diff:
        (q_ref, k_ref, v_ref, lq1_ref, lk1_ref, lq2_ref, lk2_ref, gd_ref,
         o_ref, m_ref, l_ref, acc_ref, qs_ref) = refs
    else:
        q_ref, k_ref, v_ref, o_ref, m_ref, l_ref, acc_ref = refs
    qi = pl.program_id(2)
    rows = 2 * tq if diff else tq

    m_ref[...] = jnp.full(m_ref.shape, NEG_BIG, F32)
    l_ref[...] = jnp.zeros(l_ref.shape, F32)
    acc_ref[...] = jnp.zeros(acc_ref.shape, F32)
    if diff:
        q = q_ref[0]
        lane = lax.broadcasted_iota(I32, q.shape, 1)
        qs_ref[0:tq, :] = jnp.where(lane < DIFF_HD, q, jnp.zeros_like(q))
        qs_ref[tq:, :] = jnp.where(lane >= DIFF_HD, q, jnp.zeros_like(q))

    q_start = q_off + qi * tq
    seen_by_all = ((q_start >> CHUNK_SHIFT) + 1) << CHUNK_SHIFT
    seen_by_any = (((q_start + tq - 1) >> CHUNK_SHIFT) + 1) << CHUNK_SHIFT
    n_full = jnp.minimum(seen_by_all // tk, nk)
    n_need = jnp.minimum((seen_by_any + tk - 1) // tk, nk)

    def block(j, masked):
        start = pl.multiple_of(j * tk, tk)
        q = qs_ref[...] if diff else q_ref[0]
        s = lax.dot_general(q, k_ref[0, pl.ds(start, tk), :], (((1,), (1,)), ((), ())),
                            preferred_element_type=F32)
        if masked:
            r = lax.broadcasted_iota(I32, (rows, 1), 0)
            if diff:
                r = jnp.where(r >= tq, r - tq, r)
            q_chunk = (q_start + r) >> CHUNK_SHIFT
            k_chunk = (start + lax.broadcasted_iota(I32, (1, tk), 1)) >> CHUNK_SHIFT
            s = jnp.where(k_chunk <= q_chunk, s, -jnp.inf)
        m_prev = m_ref[...]
        m_new = jnp.maximum(m_prev, jnp.max(s, axis=1, keepdims=True))
        alpha = jnp.exp2(m_prev - m_new)
        p = jnp.exp2(s - m_new)
        l_ref[...] = alpha * l_ref[...] + jnp.sum(p, axis=1, keepdims=True)
        acc_ref[...] = alpha * acc_ref[...] + jnp.dot(p.astype(BF16), v_ref[0, pl.ds(start, tk), :],
                                                      preferred_element_type=F32)
        m_ref[...] = m_new

    def full_body(j, c):
        block(j, False)
        return c

    def masked_body(j, c):
        block(j, True)
        return c

    lax.fori_loop(0, n_full, full_body, 0)
    lax.fori_loop(n_full, n_need, masked_body, 0)

    o = acc_ref[...] / l_ref[...]
    if diff:
        lam = (jnp.exp(jnp.sum(lq1_ref[...] * lk1_ref[...], keepdims=True))
               - jnp.exp(jnp.sum(lq2_ref[...] * lk2_ref[...], keepdims=True)) + lam_init)
        od = o[0:tq] - lam * o[tq:]
        o_ref[0] = (_rms(od, gd_ref[...]) * (1.0 - lam_init)).astype(o_ref.dtype)
    else:
        o_ref[0] = o.astype(o_ref.dtype)


def _attention(q, k, v, heads, tq, tk, q_off, diff_args=None, lam_init=0.0):
    b, sq, _ = q.shape
    sk = k.shape[1]
    nq, nk = sq // tq, sk // tk
    diff = diff_args is not None
    kv_spec = pl.BlockSpec((1, sk, LANES), lambda i, h, qi: (i, 0, h))
    in_specs = [pl.BlockSpec((1, tq, LANES), lambda i, h, qi: (i, qi, h)), kv_spec, kv_spec]
    rows = 2 * tq if diff else tq
    scratch = [pltpu.VMEM((rows, 1), F32), pltpu.VMEM((rows, 1), F32), pltpu.VMEM((rows, LANES), F32)]
    args = [q, k, v]
    if diff:
        small = lambda w: pl.BlockSpec((1, w), lambda i, h, qi: (0, 0))
        in_specs += [small(DIFF_HD)] * 4 + [small(2 * DIFF_HD)]
        scratch.append(pltpu.VMEM((rows, LANES), BF16))
        args += list(diff_args)
    kern = functools.partial(_attn_kernel, tq=tq, tk=tk, nk=nk, q_off=q_off, diff=diff, lam_init=lam_init)
    return pl.pallas_call(
        kern,
        grid=(b, heads, nq),
        in_specs=in_specs,
        out_specs=pl.BlockSpec((1, tq, LANES), lambda i, h, qi: (i, qi, h)),
        out_shape=jax.ShapeDtypeStruct((b, sq, heads * LANES), BF16),
        scratch_shapes=scratch,
        compiler_params=_cparams(("arbitrary",) * 3),
    )(*args)


def _out_kernel(x_ref, om_ref, od_ref, wout_ref, mod_ref, gf_ref, wr_ref, br_ref,
                x1_ref, hp_ref, idx_ref, gate_ref):
    a = (jnp.dot(om_ref[0], wout_ref[0:MLA_PAD, :], preferred_element_type=F32)
         + jnp.dot(od_ref[0], wout_ref[MLA_PAD:, :], preferred_element_type=F32))
    mod = mod_ref[0]
    x1 = x_ref[0] + mod[2:3] * a
    x1_ref[0] = x1
    h2 = _rms(x1, gf_ref[...]) * (1.0 + mod[4:5]) + mod[3:4]
    bits = lax.bitcast_convert_type(h2.astype(BF16).astype(F32), U32)
    half = h2.shape[1] // 2
    hp_ref[0] = (bits[:, :half] >> 16) | (bits[:, half:] & jnp.uint32(0xFFFF0000))

    logits = lax.dot_general(wr_ref[...], h2, (((1,), (1,)), ((), ())), precision=lax.Precision.HIGHEST,
                             preferred_element_type=F32) + br_ref[...]
    e_iota = lax.broadcasted_iota(I32, logits.shape, 0).astype(F32)
    vals, idxs = [], []
    for _ in range(TOP_K):
        m = jnp.max(logits, axis=0, keepdims=True)
        i = jnp.min(jnp.where(logits == m, e_iota, float(N_EXPERTS)), axis=0, keepdims=True)
        vals.append(m)
        idxs.append(i)
        logits = jnp.where(e_iota == i, -jnp.inf, logits)
    ex = [jnp.exp(v - vals[0]) for v in vals]
    tot = ex[0] + ex[1] + ex[2] + ex[3]
    idx_ref[0] = jnp.concatenate(idxs, axis=0).astype(I32)
    gate_ref[0] = jnp.concatenate([e / tot for e in ex], axis=0)


def _out(x, om, od, w_out_p, mod, g_ffn, w_r, b_r, t):
    b, s, d = x.shape
    ns = s // t
    full = lambda shape: pl.BlockSpec(shape, lambda i, j: (0,) * len(shape))
    tok = lambda w: pl.BlockSpec((1, t, w), lambda i, j: (i, j, 0))
    flat = pl.BlockSpec((1, TOP_K, t), lambda i, j: (i * ns + j, 0, 0))
    x1, hp, idx, gates = pl.pallas_call(
        _out_kernel,
        grid=(b, ns),
        in_specs=[tok(d), tok(MLA_PAD), tok(DIFF_QKV), full((MLA_PAD + DIFF_QKV, d)),
                  pl.BlockSpec((1, 6, d), lambda i, j: (i, 0, 0)), full((1, d)),
                  full((N_EXPERTS, d)), full((N_EXPERTS, 1))],
        out_specs=[tok(d), tok(d // 2), flat, flat],
        out_shape=[jax.ShapeDtypeStruct((b, s, d), F32), jax.ShapeDtypeStruct((b, s, d // 2), U32),
                   jax.ShapeDtypeStruct((b * ns, TOP_K, t), I32), jax.ShapeDtypeStruct((b * ns, TOP_K, t), F32)],
        compiler_params=_cparams(("arbitrary", "arbitrary")),
    )(x, om, od, w_out_p, mod, g_ffn, w_r, b_r)
    unflat = lambda a: a.transpose(1, 0, 2).reshape(TOP_K, b * s)
    return x1, hp, unflat(idx), unflat(gates)


def _rank_kernel(idx_ref, rank_ref, cnt_ref, carry_ref):
    step = pl.program_id(0)

    @pl.when(step == 0)
    def _():
        carry_ref[...] = jnp.zeros(carry_ref.shape, F32)

    idx = idx_ref[...]
    t = idx.shape[1]
    e_iota = lax.broadcasted_iota(I32, (N_EXPERTS, t), 0)
    hits = [e_iota == idx[k:k + 1, :] for k in range(TOP_K)]
    onehot = sum(h.astype(F32) for h in hits)
    earlier = (lax.broadcasted_iota(I32, (t, t), 0) < lax.broadcasted_iota(I32, (t, t), 1)).astype(BF16)
    before = jnp.dot(onehot.astype(BF16), earlier, preferred_element_type=F32) + carry_ref[...]
    rank_ref[...] = jnp.concatenate(
        [jnp.sum(jnp.where(h, before, 0.0), axis=0, keepdims=True) for h in hits], axis=0).astype(I32)
    carry_ref[...] = carry_ref[...] + jnp.sum(onehot, axis=1, keepdims=True)
    cnt_ref[...] = carry_ref[...].astype(I32)


def _rank(idx, t):
    n = idx.shape[1]
    return pl.pallas_call(
        _rank_kernel,
        grid=(n // t,),
        in_specs=[pl.BlockSpec((TOP_K, t), lambda i: (0, i))],
        out_specs=[pl.BlockSpec((TOP_K, t), lambda i: (0, i)), pl.BlockSpec((N_EXPERTS, 1), lambda i: (0, 0))],
        out_shape=[jax.ShapeDtypeStruct((TOP_K, n), I32), jax.ShapeDtypeStruct((N_EXPERTS, 1), I32)],
        scratch_shapes=[pltpu.VMEM((N_EXPERTS, 1), F32)],
        compiler_params=_cparams(("arbitrary",)),
    )(idx)


def _dispatch_kernel(dest_ref, hp_ref, xs_in_ref, xs_ref, dest_smem, sem, idx_sem, *, t):
    del xs_in_ref
    cp = pltpu.make_async_copy(dest_ref.at[0], dest_smem, idx_sem)
    cp.start()
    cp.wait()

    def row_copy(tok, k):
        return pltpu.make_async_copy(hp_ref.at[pl.ds(tok, 1)], xs_ref.at[pl.ds(dest_smem[k, tok], 1)], sem)

    def issue(tok, c):
        for k in range(TOP_K):
            row_copy(tok, k).start()
        return c

    lax.fori_loop(0, t, issue, 0, unroll=ISSUE_UNROLL)
    for k in range(TOP_K):
        pltpu.make_async_copy(hp_ref, xs_ref.at[pl.ds(0, t)], sem).wait()


def _dispatch(dest3, hp, n_slots, t):
    n, w = hp.shape
    zeros = jnp.zeros((n_slots, w), U32)
    return pl.pallas_call(
        functools.partial(_dispatch_kernel, t=t),
        grid=(n // t,),
        in_specs=[pl.BlockSpec((1, TOP_K, t), lambda i: (i, 0, 0)),
                  pl.BlockSpec((t, w), lambda i: (i, 0)),
                  pl.BlockSpec(memory_space=pl.ANY)],
        out_specs=pl.BlockSpec(memory_space=pl.ANY),
        out_shape=jax.ShapeDtypeStruct((n_slots, w), U32),
        scratch_shapes=[pltpu.SMEM((TOP_K, t), I32), pltpu.SemaphoreType.DMA, pltpu.SemaphoreType.DMA],
        input_output_aliases={2: 0},
        compiler_params=_cparams(("arbitrary",)),
    )(dest3, hp, zeros)


def _expert_kernel(te_ref, nu_ref, xs_ref, wgu_ref, bgu_ref, wd_ref, bd_ref, ys_ref):
    del te_ref

    @pl.when(pl.program_id(0) < nu_ref[0])
    def _():
        p = xs_ref[...]
        x_lo = lax.bitcast_convert_type(p << 16, F32).astype(BF16)
        x_hi = lax.bitcast_convert_type(p & jnp.uint32(0xFFFF0000), F32).astype(BF16)
        half = p.shape[1]
        gu = (jnp.dot(x_lo, wgu_ref[0, 0:half, :], preferred_element_type=F32)
              + jnp.dot(x_hi, wgu_ref[0, half:, :], preferred_element_type=F32) + bgu_ref[0])
        f = gu.shape[1] // 2
        glu = jnp.minimum(gu[:, :f], SWIGLU_LIMIT)
        lin = jnp.clip(gu[:, f:], -SWIGLU_LIMIT, SWIGLU_LIMIT)
        act = glu * jax.nn.sigmoid(SWIGLU_ALPHA * glu) * (lin + 1.0)
        ys_ref[...] = jnp.dot(act.astype(BF16), wd_ref[0], preferred_element_type=F32) + bd_ref[0]


def _experts(tile_e, n_used, xs, w_gu, b_gu, w_down, b_down):
    n_slots, half = xs.shape
    e, d, f2 = w_gu.shape
    n_tiles = n_slots // MOE_TILE
    row = lambda i, te, nu: (jnp.minimum(i, nu[0] - 1), 0)
    wsel = lambda i, te, nu: (te[i], 0, 0)
    grid_spec = pltpu.PrefetchScalarGridSpec(
        num_scalar_prefetch=2,
        grid=(n_tiles,),
        in_specs=[pl.BlockSpec((MOE_TILE, half), row),
                  pl.BlockSpec((1, d, f2), wsel), pl.BlockSpec((1, 1, f2), wsel),
                  pl.BlockSpec((1, f2 // 2, d), wsel), pl.BlockSpec((1, 1, d), wsel)],
        out_specs=pl.BlockSpec((MOE_TILE, d), row),
    )
    return pl.pallas_call(
        _expert_kernel,
        grid_spec=grid_spec,
        out_shape=jax.ShapeDtypeStruct((n_slots, d), F32),
        compiler_params=_cparams(("arbitrary",)),
    )(tile_e, n_used, xs, w_gu, b_gu.reshape(e, 1, f2), w_down, b_down.reshape(e, 1, d))


def _combine_kernel(dcur_ref, dnxt_ref, gate_ref, x1_ref, mod_ref, gfin_ref, ys_ref,
                    x2_ref, buf, dsm, sem, idx_sem, *, t, n_steps, final):
    i = pl.program_id(0)
    slot = i % 2

    def load_idx(src, s):
        cp = pltpu.make_async_copy(src.at[0], dsm.at[s], idx_sem)
        cp.start()
        cp.wait()

    def row_copy(s, tok, k):
        return pltpu.make_async_copy(ys_ref.at[pl.ds(dsm[s, k, tok], 1)], buf.at[s, k, pl.ds(tok, 1)], sem.at[s])

    def issue(s):
        def body(tok, c):
            for k in range(TOP_K):
                row_copy(s, tok, k).start()
            return c
        lax.fori_loop(0, t, body, 0, unroll=ISSUE_UNROLL)

    @pl.when(i == 0)
    def _():
        load_idx(dcur_ref, 0)
        issue(0)

    @pl.when(i + 1 < n_steps)
    def _():
        load_idx(dnxt_ref, 1 - slot)
        issue(1 - slot)

    for k in range(TOP_K):
        pltpu.make_async_copy(ys_ref.at[pl.ds(0, t)], buf.at[slot, k], sem.at[slot]).wait()

    g = gate_ref[...]
    y = g[:, 0:1] * buf[slot, 0]
    for k in range(1, TOP_K):
        y = y + g[:, k:k + 1] * buf[slot, k]
    x2 = x1_ref[...] + mod_ref[0][5:6] * y
    x2_ref[...] = _rms(x2, gfin_ref[...]) if final else x2


def _combine(dest3, gates_t, x1, mod, g_final, ys, s, t, final):
    n, d = x1.shape
    n_steps = n // t
    per_b = s // t
    return pl.pallas_call(
        functools.partial(_combine_kernel, t=t, n_steps=n_steps, final=final),
        grid=(n_steps,),
        in_specs=[pl.BlockSpec((1, TOP_K, t), lambda i: (i, 0, 0)),
                  pl.BlockSpec((1, TOP_K, t), lambda i: (jnp.minimum(i + 1, n_steps - 1), 0, 0)),
                  pl.BlockSpec((t, TOP_K), lambda i: (i, 0)),
                  pl.BlockSpec((t, d), lambda i: (i, 0)),
                  pl.BlockSpec((1, 6, d), lambda i: (i // per_b, 0, 0)),
                  pl.BlockSpec((1, d), lambda i: (0, 0)),
                  pl.BlockSpec(memory_space=pl.ANY)],
        out_specs=pl.BlockSpec((t, d), lambda i: (i, 0)),
        out_shape=jax.ShapeDtypeStruct((n, d), F32),
        scratch_shapes=[pltpu.VMEM((2, TOP_K, t, d), F32), pltpu.SMEM((2, TOP_K, t), I32),
                        pltpu.SemaphoreType.DMA((2,)), pltpu.SemaphoreType.DMA],
        compiler_params=_cparams(("arbitrary",)),
    )(dest3, dest3, gates_t, x1, mod, g_final, ys)


def _rope_tables(pos, d, lane_pos):
    inv = ROPE_THETA ** (-jnp.arange(0, d, 2, dtype=F32) / d)
    ang = pos.astype(F32)[:, None] * inv[None, :]
    cos, sin = jnp.cos(ang), jnp.sin(ang)
    cos_t = jnp.ones((pos.shape[0], LANES), F32)
    sin_t = jnp.zeros((pos.shape[0], LANES), F32)
    for o in lane_pos:
        cos_t = cos_t.at[:, o:o + d].set(jnp.concatenate([cos, cos], axis=1))
        sin_t = sin_t.at[:, o:o + d].set(jnp.concatenate([-sin, sin], axis=1))
    return cos_t, sin_t


def _prep_weights(w_in, w_uq, w_uk, w_uv, w_out):
    depth, d, _ = w_in.shape
    cuts = [MLA_Q_LORA, MLA_KV_LORA, MLA_ROPE, DIFF_QKV, DIFF_QKV, DIFF_QKV]
    offs = [0]
    for c in cuts:
        offs.append(offs[-1] + c)
    parts = [w_in[:, :, offs[i]:offs[i + 1]] for i in range(6)]
    kr_pad = jnp.pad(parts[2], ((0, 0), (0, 0), (MLA_NOPE, LANES - MLA_NOPE - MLA_ROPE)))
    w_in_p = jnp.concatenate([parts[0], parts[1], kr_pad, parts[3], parts[4], parts[5]], axis=2).astype(BF16)
    qd = MLA_NOPE + MLA_ROPE
    w_uq_p = jnp.pad(w_uq.reshape(depth, MLA_Q_LORA, MLA_HEADS, qd), ((0, 0), (0, 0), (0, 0), (0, LANES - qd)))
    w_uq_p = w_uq_p.reshape(depth, MLA_Q_LORA, MLA_PAD).astype(BF16)
    pad_head = lambda w, n: jnp.pad(w, ((0, 0), (0, 0), (0, 0), (0, LANES - n))).reshape(depth, MLA_KV_LORA, MLA_PAD)
    w_ukv_p = jnp.concatenate([pad_head(w_uk, MLA_NOPE), pad_head(w_uv, MLA_V)], axis=2).astype(BF16)
    wo_m = jnp.pad(w_out[:, :MLA_HEADS * MLA_V].reshape(depth, MLA_HEADS, MLA_V, d),
                   ((0, 0), (0, 0), (0, LANES - MLA_V), (0, 0))).reshape(depth, MLA_PAD, d)
    w_out_p = jnp.concatenate([wo_m, w_out[:, MLA_HEADS * MLA_V:]], axis=1).astype(BF16)
    place = jnp.zeros((MLA_ROPE, 2 * MLA_PAD), F32)
    eye = jnp.eye(MLA_ROPE, dtype=F32)
    for hd in range(MLA_HEADS):
        place = place.at[:, hd * LANES + MLA_NOPE:hd * LANES + MLA_NOPE + MLA_ROPE].set(eye)
    w_uvt_p = jnp.swapaxes(w_ukv_p[:, :, MLA_PAD:], 1, 2)
    return w_in_p, w_uq_p, w_ukv_p, w_out_p, place.astype(BF16), w_uvt_p


def _pick(n, pref):
    t = min(n, pref)
    while n % t:
        t //= 2
    return t


def _moe(hp, idx, gates, x1, mod_l, g_final, wts, s, final):
    w_gu, b_gu, w_down, b_down = wts
    n = hp.shape[0]
    rank, counts = _rank(idx, _pick(n, RANK_TILE))
    counts = counts[:, 0]
    padded = (counts + MOE_TILE - 1) // MOE_TILE * MOE_TILE
    pad_ends = jnp.cumsum(padded)
    pad_starts = pad_ends - padded
    dest = rank + jnp.sum(jnp.where(idx[None] == jnp.arange(N_EXPERTS, dtype=I32)[:, None, None],
                                    pad_starts[:, None, None], 0), axis=0)
    n_tiles = (n * TOP_K) // MOE_TILE + N_EXPERTS
    tile_e = jnp.minimum(jnp.sum(pad_ends[None, :] <= (jnp.arange(n_tiles, dtype=I32) * MOE_TILE)[:, None], axis=1),
                         N_EXPERTS - 1).astype(I32)
    n_used = (pad_ends[-1:] // MOE_TILE).astype(I32)
    td = _pick(n, DISPATCH_TILE)
    xs = _dispatch(dest.reshape(TOP_K, n // td, td).transpose(1, 0, 2), hp, n_tiles * MOE_TILE, td)
    ys = _experts(tile_e, n_used, xs, w_gu, b_gu, w_down, b_down)
    tc = _pick(s, COMBINE_TILE)
    return _combine(dest.reshape(TOP_K, n // tc, tc).transpose(1, 0, 2), gates.T, x1, mod_l, g_final, ys, s, tc,
                    final)


def _trunk(x, c, caches, prm, prepped, q_off):
    b, s, d = x.shape
    depth = prm['w_in'].shape[0]
    w_in_p, w_uq_p, w_ukv_p, w_out_p, place, w_uvt_p, w_ada_b, w_gu_b, w_down_b = prepped
    rows = -(-b // 16) * 16
    mod = _ada(jnp.pad(c, ((0, rows - b), (0, 0))), w_ada_b, prm['b_ada'])[:, :b].reshape(depth, b, 6, d)
    pos = q_off + jnp.arange(s, dtype=I32)
    cos_q, sin_q = _rope_tables(pos, MLA_ROPE, (MLA_NOPE,))
    cos_d, sin_d = _rope_tables(pos, DIFF_HD, (0, DIFF_HD))
    t = _pick(s, TOK_TILE)
    new_rows = ([], [], [], [])
    row = lambda v: v.reshape(1, -1)
    for l in range(depth):
        qm, qd, lat, kr, dk, dv = _proj(x, mod[l], row(prm['g_attn'][l]), w_in_p[l], row(prm['g_q'][l]), w_uq_p[l],
                                        row(prm['g_kv'][l]), (cos_q, sin_q, cos_d, sin_d), t)
        for lst, r in zip(new_rows, (lat, kr, dk, dv)):
            lst.append(r)
        if caches is not None:
            lat_a, kr_a, dk_a, dv_a = (jnp.concatenate([cc[l].reshape(b, cc.shape[2], -1), r], axis=1)
                                       for cc, r in zip(caches, (lat, kr, dk, dv)))
        else:
            lat_a, kr_a, dk_a, dv_a = lat, kr, dk, dv
        sk = lat_a.shape[1]
        lam_init = 0.8 - 0.6 * math.exp(-0.3 * l)
        diff_args = (row(prm['lam_q1'][l]), row(prm['lam_k1'][l]), row(prm['lam_q2'][l]),
                     row(prm['lam_k2'][l]), row(prm['g_diff'][l]))
        if caches is None and q_off == 0 and sk % ATTN_TK == 0:
            km, vm_t, dkb, dv_t = _kvt(lat_a, kr_a, dk_a, dv_a, w_ukv_p[l], place, w_uvt_p[l], ATTN_TK)
            om = _attention_t(qm, km, vm_t, MLA_HEADS, _pick(s, ATTN_TQ))
            od = _attention_t(qd, dkb, dv_t, DIFF_HEADS, _pick(s, ATTN_TQ // 2), diff_args, lam_init)
        else:
            tk = ATTN_TK if sk % ATTN_TK == 0 else sk
            tkv = max(t_ for t_ in range(16, min(sk, 2 * TOK_TILE) + 1, 16) if sk % t_ == 0)
            km, vm, dkb, dvb = _kv(lat_a, kr_a, dk_a, dv_a, w_ukv_p[l], place, tkv)
            om = _attention(qm, km, vm, MLA_HEADS, _pick(s, ATTN_TQ), tk, q_off)
            od = _attention(qd, dkb, dvb, DIFF_HEADS, _pick(s, ATTN_TQ // 2), tk, q_off, diff_args, lam_init)
        x1, hp, idx, gates = _out(x, om, od, w_out_p[l], mod[l], row(prm['g_ffn'][l]),
                                  prm['w_router'][l].T, prm['b_router'][l].reshape(-1, 1), t)
        x = _moe(hp.reshape(b * s, d // 2), idx, gates, x1.reshape(b * s, d), mod[l], row(prm['g_final']),
                 (w_gu_b[l], prm['b_gate_up'][l], w_down_b[l], prm['b_down'][l]), s, l == depth - 1).reshape(b, s, d)
    lat_n, kr_n, dk_n, dv_n = (jnp.stack(lst) for lst in new_rows)
    shp = (depth, b, s, DIFF_HEADS, 2 * DIFF_HD)
    return x, (lat_n, kr_n, dk_n.reshape(shp), dv_n.reshape(shp))


def kernel(x_prompt, x_sample, c_prompt, c_sample, cache_mla_latent, cache_mla_krope, cache_diff_k, cache_diff_v,
           w_ada, b_ada, g_attn, g_ffn, w_in, g_q, w_uq, g_kv, w_uk, w_uv, lam_q1, lam_k1, lam_q2, lam_k2, g_diff,
           w_out, w_router, b_router, w_gate_up, b_gate_up, w_down, b_down, g_final):
    prm = {'b_ada': b_ada, 'g_attn': g_attn, 'g_ffn': g_ffn, 'w_in': w_in, 'g_q': g_q, 'g_kv': g_kv,
           'lam_q1': lam_q1, 'lam_k1': lam_k1, 'lam_q2': lam_q2, 'lam_k2': lam_k2, 'g_diff': g_diff,
           'w_router': w_router, 'b_router': b_router, 'b_gate_up': b_gate_up, 'b_down': b_down,
           'g_final': g_final}
    prepped = _prep_weights(w_in, w_uq, w_uk, w_uv, w_out) + (
        w_ada.astype(BF16), w_gate_up.astype(BF16), w_down.astype(BF16))
    y_p, rows_p = _trunk(x_prompt, c_prompt, None, prm, prepped, 0)
    caches = (cache_mla_latent, cache_mla_krope, cache_diff_k, cache_diff_v)
    y_s, rows_s = _trunk(x_sample, c_sample, caches, prm, prepped, cache_mla_latent.shape[2])
    return (y_p, y_s) + rows_p + rows_s
```

```python
import functools
import math

import jax
import jax.numpy as jnp
from jax import lax
from jax.experimental import pallas as pl
from jax.experimental.pallas import tpu as pltpu

F32 = jnp.float32
BF16 = jnp.bfloat16
I32 = jnp.int32
U32 = jnp.uint32

CHUNK = 64
ROPE_THETA = 10000.0
EPS = 1e-6
MLA_HEADS = 8
MLA_Q_LORA = 256
MLA_KV_LORA = 128
MLA_NOPE = 64
MLA_ROPE = 32
MLA_V = 64
DIFF_HEADS = 4
DIFF_HD = 64
DIFF_QKV = DIFF_HEADS * 2 * DIFF_HD
N_EXPERTS = 32
TOP_K = 4
SWIGLU_LIMIT = 7.0
SWIGLU_ALPHA = 1.702

LANES = 128
MLA_PAD = MLA_HEADS * LANES
PROJ_COLS = MLA_Q_LORA + MLA_KV_LORA + LANES + 3 * DIFF_QKV
VMEM_LIMIT = 56 * 1024 * 1024
MOE_TILE = 512
MOE_TILE_SMALL = 128
TOK_TILE = 512
ATTN_TQ = 1024
ATTN_TK = 512
RANK_TILE = 512
DISPATCH_TILE = 512
COMBINE_TILE = 256
ISSUE_UNROLL = 4
NEG_BIG = -1e30
LOG2E = math.log2(math.e)
assert CHUNK & (CHUNK - 1) == 0
CHUNK_SHIFT = CHUNK.bit_length() - 1


def _cparams(sem):
    return pltpu.CompilerParams(dimension_semantics=sem, vmem_limit_bytes=VMEM_LIMIT)


def _rms(x, g):
    return x * lax.rsqrt(jnp.mean(x * x, axis=-1, keepdims=True) + EPS) * g


def _rope_lanes(v, cos, sin_signed, half):
    lane = lax.broadcasted_iota(I32, v.shape, 1)
    first = (lane % (2 * half)) < half
    swapped = jnp.where(first, pltpu.roll(v, LANES - half, 1), pltpu.roll(v, half, 1))
    return v * cos + swapped * sin_signed


def _ada_kernel(c_ref, w_ref, b_ref, o_ref):
    c = c_ref[...]
    s = c * jax.nn.sigmoid(c)
    o_ref[0] = jnp.dot(s.astype(BF16), w_ref[0], preferred_element_type=F32) + b_ref[0]


def _ada(c, w_ada, b_ada):
    depth, d, n6 = w_ada.shape
    rows = c.shape[0]
    tn = 1536
    return pl.pallas_call(
        _ada_kernel,
        grid=(depth, n6 // tn),
        in_specs=[pl.BlockSpec((rows, d), lambda l, j: (0, 0)),
                  pl.BlockSpec((1, d, tn), lambda l, j: (l, 0, j)),
                  pl.BlockSpec((1, 1, tn), lambda l, j: (l, 0, j))],
        out_specs=pl.BlockSpec((1, rows, tn), lambda l, j: (l, 0, j)),
        out_shape=jax.ShapeDtypeStruct((depth, rows, n6), F32),
        compiler_params=_cparams(("arbitrary", "arbitrary")),
    )(c, w_ada, b_ada.reshape(depth, 1, n6))


def _proj_kernel(x_ref, mod_ref, ga_ref, win_ref, gq_ref, wuq_ref, gkv_ref,
                 cq_ref, sq_ref, cd_ref, sd_ref, *rest):
    qm_ref, qd_ref, lat_ref, kr_ref, dk_ref, dv_ref = rest[-6:]
    x = x_ref[0]
    mod = mod_ref[0]
    h = _rms(x, ga_ref[...]) * (1.0 + mod[1:2]) + mod[0:1]
    proj = jnp.dot(h.astype(BF16), win_ref[...], preferred_element_type=F32)
    o = 0
    cq = proj[:, o:o + MLA_Q_LORA]; o += MLA_Q_LORA
    ckv = proj[:, o:o + MLA_KV_LORA]; o += MLA_KV_LORA
    krp = proj[:, o:o + LANES]; o += LANES
    dq = proj[:, o:o + DIFF_QKV]; o += DIFF_QKV
    dk = proj[:, o:o + DIFF_QKV]; o += DIFF_QKV
    dv = proj[:, o:o + DIFF_QKV]

    cos_q, sin_q = cq_ref[...], sq_ref[...]
    cos_d, sin_d = cd_ref[...], sd_ref[...]
    q = jnp.dot(_rms(cq, gq_ref[...]).astype(BF16), wuq_ref[...], preferred_element_type=F32)
    q_scale = LOG2E / math.sqrt(MLA_NOPE + MLA_ROPE)
    for hd in range(MLA_HEADS):
        sl = slice(hd * LANES, (hd + 1) * LANES)
        qm_ref[0, :, sl] = (_rope_lanes(q[:, sl], cos_q, sin_q, MLA_ROPE // 2) * q_scale).astype(BF16)
    lat_ref[0, 0] = _rms(ckv, gkv_ref[...])
    kr_ref[0, 0] = _rope_lanes(krp, cos_q, sin_q, MLA_ROPE // 2)[:, MLA_NOPE:MLA_NOPE + MLA_ROPE]
    d_scale = LOG2E / math.sqrt(DIFF_HD)
    for hd in range(DIFF_HEADS):
        sl = slice(hd * LANES, (hd + 1) * LANES)
        qd_ref[0, :, sl] = (_rope_lanes(dq[:, sl], cos_d, sin_d, DIFF_HD // 2) * d_scale).astype(BF16)
        dk_ref[0, 0, :, sl] = _rope_lanes(dk[:, sl], cos_d, sin_d, DIFF_HD // 2)
    dv_ref[0, 0] = dv


ROW_WIDTHS = (MLA_KV_LORA, MLA_ROPE, DIFF_QKV, DIFF_QKV)


def _proj(x, mod, g_attn, w_in_p, g_q, w_uq_p, g_kv, tabs, t, layer, depth, stacks):
    b, s, d = x.shape
    cos_q, sin_q, cos_d, sin_d = tabs
    full = lambda shape: pl.BlockSpec(shape, lambda i, j: (0,) * len(shape))
    tok = lambda w: pl.BlockSpec((1, t, w), lambda i, j: (i, j, 0))
    tab = pl.BlockSpec((t, LANES), lambda i, j: (j, 0))
    lay = lambda w: pl.BlockSpec((1, 1, t, w), lambda i, j: (layer, i, j, 0))
    n_in = 11
    in_specs = [tok(d), pl.BlockSpec((1, 6, d), lambda i, j: (i, 0, 0)), full((1, d)),
                full((d, PROJ_COLS)), full((1, MLA_Q_LORA)), full((MLA_Q_LORA, MLA_PAD)),
                full((1, MLA_KV_LORA)), tab, tab, tab, tab]
    args = [x, mod, g_attn, w_in_p, g_q, w_uq_p, g_kv, cos_q, sin_q, cos_d, sin_d]
    aliases = {}
    if stacks is not None:
        in_specs += [pl.BlockSpec(memory_space=pl.ANY)] * len(ROW_WIDTHS)
        args += list(stacks)
        aliases = {n_in + k: 2 + k for k in range(len(ROW_WIDTHS))}
    out = pl.pallas_call(
        _proj_kernel,
        grid=(b, s // t),
        in_specs=in_specs,
        out_specs=[tok(MLA_PAD), tok(DIFF_QKV)] + [lay(w) for w in ROW_WIDTHS],
        out_shape=[jax.ShapeDtypeStruct((b, s, MLA_PAD), BF16), jax.ShapeDtypeStruct((b, s, DIFF_QKV), BF16)]
        + [jax.ShapeDtypeStruct((depth, b, s, w), F32) for w in ROW_WIDTHS],
        input_output_aliases=aliases,
        compiler_params=_cparams(("arbitrary", "arbitrary")),
    )(*args)
    return out[0], out[1], tuple(out[2:])


def _kv_kernel(lat_ref, kr_ref, dk_ref, dv_ref, wukv_ref, place_ref, km_ref, vm_ref, dkb_ref, dvb_ref):
    kv = jnp.dot(lat_ref[0].astype(BF16), wukv_ref[...], preferred_element_type=F32)
    kv = kv + jnp.dot(kr_ref[0].astype(BF16), place_ref[...], preferred_element_type=F32)
    km_ref[0] = kv[:, :MLA_PAD].astype(BF16)
    vm_ref[0] = kv[:, MLA_PAD:].astype(BF16)
    dkb_ref[0] = dk_ref[0].astype(BF16)
    dvb_ref[0] = dv_ref[0].astype(BF16)


def _kv(lat, kr, dk, dv, w_ukv_p, place, t):
    b, s, _ = lat.shape
    full = lambda shape: pl.BlockSpec(shape, lambda i, j: (0,) * len(shape))
    tok = lambda w: pl.BlockSpec((1, t, w), lambda i, j: (i, j, 0))
    return pl.pallas_call(
        _kv_kernel,
        grid=(b, s // t),
        in_specs=[tok(MLA_KV_LORA), tok(MLA_ROPE), tok(DIFF_QKV), tok(DIFF_QKV),
                  full((MLA_KV_LORA, 2 * MLA_PAD)), full((MLA_ROPE, 2 * MLA_PAD))],
        out_specs=[tok(MLA_PAD), tok(MLA_PAD), tok(DIFF_QKV), tok(DIFF_QKV)],
        out_shape=[jax.ShapeDtypeStruct((b, s, MLA_PAD), BF16), jax.ShapeDtypeStruct((b, s, MLA_PAD), BF16),
                   jax.ShapeDtypeStruct((b, s, DIFF_QKV), BF16), jax.ShapeDtypeStruct((b, s, DIFF_QKV), BF16)],
        compiler_params=_cparams(("arbitrary", "arbitrary")),
    )(lat, kr, dk, dv, w_ukv_p, place)


def _kvt_kernel(lat_ref, kr_ref, dk_ref, dv_ref, wukv_ref, place_ref, wuvt_ref, eye_ref,
                km_ref, vt_ref, dkb_ref, dvt_ref):
    lat = lat_ref[0, 0].astype(BF16)
    k = (jnp.dot(lat, wukv_ref[:, :MLA_PAD], preferred_element_type=F32)
         + jnp.dot(kr_ref[0, 0].astype(BF16), place_ref[:, :MLA_PAD], preferred_element_type=F32))
    km_ref[0] = k.astype(BF16)
    nt = (((1,), (1,)), ((), ()))
    v_t = lax.dot_general(wuvt_ref[...], lat, nt, preferred_element_type=F32)
    for hd in range(MLA_HEADS):
        vt_ref[0, hd, 0] = v_t[hd * LANES:(hd + 1) * LANES, :].astype(BF16)
    dkb_ref[0] = dk_ref[0, 0].astype(BF16)
    dv_t = lax.dot_general(eye_ref[...], dv_ref[0, 0].astype(BF16), nt, preferred_element_type=F32)
    for hd in range(DIFF_HEADS):
        dvt_ref[0, hd, 0] = dv_t[hd * LANES:(hd + 1) * LANES, :].astype(BF16)


def _kvt(stacks, layer, w_ukv_p, place, w_uvt_p, t):
    _, b, s, _ = stacks[0].shape
    nk = s // t
    full = lambda shape: pl.BlockSpec(shape, lambda i, j: (0,) * len(shape))
    tok = lambda w: pl.BlockSpec((1, t, w), lambda i, j: (i, j, 0))
    lay = lambda w: pl.BlockSpec((1, 1, t, w), lambda i, j: (layer, i, j, 0))
    tr = lambda h: pl.BlockSpec((1, h, 1, LANES, t), lambda i, j: (i, 0, j, 0, 0))
    eye = jnp.eye(DIFF_QKV, dtype=BF16)
    lat, kr, dk, dv = stacks
    return pl.pallas_call(
        _kvt_kernel,
        grid=(b, nk),
        in_specs=[lay(w) for w in ROW_WIDTHS]
        + [full((MLA_KV_LORA, 2 * MLA_PAD)), full((MLA_ROPE, 2 * MLA_PAD)),
           full((MLA_PAD, MLA_KV_LORA)), full((DIFF_QKV, DIFF_QKV))],
        out_specs=[tok(MLA_PAD), tr(MLA_HEADS), tok(DIFF_QKV), tr(DIFF_HEADS)],
        out_shape=[jax.ShapeDtypeStruct((b, s, MLA_PAD), BF16),
                   jax.ShapeDtypeStruct((b, MLA_HEADS, nk, LANES, t), BF16),
                   jax.ShapeDtypeStruct((b, s, DIFF_QKV), BF16),
                   jax.ShapeDtypeStruct((b, DIFF_HEADS, nk, LANES, t), BF16)],
        compiler_params=_cparams(("arbitrary", "arbitrary")),
    )(lat, kr, dk, dv, w_ukv_p, place, w_uvt_p, eye)


def _attn_t_kernel(*refs, tq, tk, nk, diff, lam_init):
    if diff:
        (q_ref, k_ref, vt_ref, lq1_ref, lk1_ref, lq2_ref, lk2_ref, gd_ref,
         o_ref, m_ref, l_ref, acc_ref, sa_ref, sb_ref, qs_ref) = refs
    else:
        q_ref, k_ref, vt_ref, o_ref, m_ref, l_ref, acc_ref, sa_ref, sb_ref = refs
    qi = pl.program_id(2)
    cols = 2 * tq if diff else tq

    m_ref[...] = jnp.full(m_ref.shape, NEG_BIG, F32)
    l_ref[...] = jnp.zeros(l_ref.shape, F32)
    acc_ref[...] = jnp.zeros(acc_ref.shape, F32)
    if diff:
        q = q_ref[0]
        lane = lax.broadcasted_iota(I32, q.shape, 1)
        qs_ref[0:tq, :] = jnp.where(lane < DIFF_HD, q, jnp.zeros_like(q))
        qs_ref[tq:, :] = jnp.where(lane >= DIFF_HD, q, jnp.zeros_like(q))

    q_start = qi * tq
    seen_by_all = ((q_start >> CHUNK_SHIFT) + 1) << CHUNK_SHIFT
    seen_by_any = (((q_start + tq - 1) >> CHUNK_SHIFT) + 1) << CHUNK_SHIFT
    n_full = jnp.minimum(seen_by_all // tk, nk)
    n_need = jnp.minimum((seen_by_any + tk - 1) // tk, nk)

    def scores(j):
        start = pl.multiple_of(jnp.minimum(j, nk - 1) * tk, tk)
        q = qs_ref[...] if diff else q_ref[0]
        return lax.dot_general(k_ref[0, pl.ds(start, tk), :], q, (((1,), (1,)), ((), ())),
                               preferred_element_type=F32)

    def absorb(s_ref, j, masked):
        s = s_ref[...]
        if masked:
            c = lax.broadcasted_iota(I32, (1, cols), 1)
            if diff:
                c = jnp.where(c >= tq, c - tq, c)
            q_chunk = (q_start + c) >> CHUNK_SHIFT
            k_chunk = (j * tk + lax.broadcasted_iota(I32, (tk, 1), 0)) >> CHUNK_SHIFT
            s = jnp.where(k_chunk <= q_chunk, s, -jnp.inf)
        m_prev = m_ref[...]
        m_new = jnp.maximum(m_prev, jnp.max(s, axis=0, keepdims=True))
        alpha = jnp.exp2(m_prev - m_new)
        p = jnp.exp2(s - m_new)
        l_ref[...] = alpha * l_ref[...] + jnp.sum(p, axis=0, keepdims=True)
        acc_ref[...] = alpha * acc_ref[...] + jnp.dot(vt_ref[0, 0, j], p.astype(BF16), preferred_element_type=F32)
        m_ref[...] = m_new

    sa_ref[...] = scores(0)

    def pair(i, carry):
        j = 2 * i
        sb_ref[...] = scores(j + 1)
        absorb(sa_ref, j, False)
        sa_ref[...] = scores(j + 2)
        absorb(sb_ref, j + 1, False)
        return carry

    n_pairs = n_full // 2
    lax.fori_loop(0, n_pairs, pair, 0)
    done = 2 * n_pairs
    left = n_need - done

    @pl.when(left >= 1)
    def _():
        sb_ref[...] = scores(done + 1)
        absorb(sa_ref, done, True)

    @pl.when(left >= 2)
    def _():
        absorb(sb_ref, done + 1, True)

    def rest(j, carry):
        sa_ref[...] = scores(j)
        absorb(sa_ref, j, True)
        return carry

    lax.fori_loop(done + 2, n_need, rest, 0)

    o_t = acc_ref[...] / l_ref[...]
    if diff:
        lam = (jnp.exp(jnp.sum(lq1_ref[...] * lk1_ref[...], keepdims=True))
               - jnp.exp(jnp.sum(lq2_ref[...] * lk2_ref[...], keepdims=True)) + lam_init)
        od = o_t[:, 0:tq] - lam * o_t[:, tq:]
        od = od * lax.rsqrt(jnp.mean(od * od, axis=0, keepdims=True) + EPS) * gd_ref[...]
        o_ref[0] = (od * (1.0 - lam_init)).T.astype(o_ref.dtype)
    else:
        o_ref[0] = o_t.T.astype(o_ref.dtype)


def _attention_t(q, k, v_t, heads, tq, diff_args=None, lam_init=0.0):
    b, sq, _ = q.shape
    nk, tk = v_t.shape[2], v_t.shape[4]
    diff = diff_args is not None
    in_specs = [pl.BlockSpec((1, tq, LANES), lambda i, h, qi: (i, qi, h)),
                pl.BlockSpec((1, nk * tk, LANES), lambda i, h, qi: (i, 0, h)),
                pl.BlockSpec((1, 1, nk, LANES, tk), lambda i, h, qi: (i, h, 0, 0, 0))]
    cols = 2 * tq if diff else tq
    scratch = [pltpu.VMEM((1, cols), F32), pltpu.VMEM((1, cols), F32), pltpu.VMEM((LANES, cols), F32),
               pltpu.VMEM((tk, cols), F32), pltpu.VMEM((tk, cols), F32)]
    args = [q, k, v_t]
    if diff:
        lq1, lk1, lq2, lk2, g_diff = diff_args
        small = pl.BlockSpec((1, DIFF_HD), lambda i, h, qi: (0, 0))
        in_specs += [small] * 4 + [pl.BlockSpec((2 * DIFF_HD, 1), lambda i, h, qi: (0, 0))]
        scratch.append(pltpu.VMEM((cols, LANES), BF16))
        args += [lq1, lk1, lq2, lk2, g_diff.reshape(-1, 1)]
    kern = functools.partial(_attn_t_kernel, tq=tq, tk=tk, nk=nk, diff=diff, lam_init=lam_init)
    return pl.pallas_call(
        kern,
        grid=(b, heads, sq // tq),
        in_specs=in_specs,
        out_specs=pl.BlockSpec((1, tq, LANES), lambda i, h, qi: (i, qi, h)),
        out_shape=jax.ShapeDtypeStruct((b, sq, heads * LANES), BF16),
        scratch_shapes=scratch,
        compiler_params=_cparams(("arbitrary",) * 3),
    )(*args)


def _attn_kernel(*refs, tq, tk, nk, q_off, diff, lam_init):
    if diff:
        (q_ref, k_ref, v_ref, lq1_ref, lk1_ref, lq2_ref, lk2_ref, gd_ref,
         o_ref, m_ref, l_ref, acc_ref, qs_ref) = refs
    else:
        q_ref, k_ref, v_ref, o_ref, m_ref, l_ref, acc_ref = refs
    qi = pl.program_id(2)
    rows = 2 * tq if diff else tq

    m_ref[...] = jnp.full(m_ref.shape, NEG_BIG, F32)
    l_ref[...] = jnp.zeros(l_ref.shape, F32)
    acc_ref[...] = jnp.zeros(acc_ref.shape, F32)
    if diff:
        q = q_ref[0]
        lane = lax.broadcasted_iota(I32, q.shape, 1)
        qs_ref[0:tq, :] = jnp.where(lane < DIFF_HD, q, jnp.zeros_like(q))
        qs_ref[tq:, :] = jnp.where(lane >= DIFF_HD, q, jnp.zeros_like(q))

    q_start = q_off + qi * tq
    seen_by_all = ((q_start >> CHUNK_SHIFT) + 1) << CHUNK_SHIFT
    seen_by_any = (((q_start + tq - 1) >> CHUNK_SHIFT) + 1) << CHUNK_SHIFT
    n_full = jnp.minimum(seen_by_all // tk, nk)
    n_need = jnp.minimum((seen_by_any + tk - 1) // tk, nk)

    def block(j, masked):
        start = pl.multiple_of(j * tk, tk)
        q = qs_ref[...] if diff else q_ref[0]
        s = lax.dot_general(q, k_ref[0, pl.ds(start, tk), :], (((1,), (1,)), ((), ())),
                            preferred_element_type=F32)
        if masked:
            r = lax.broadcasted_iota(I32, (rows, 1), 0)
            if diff:
                r = jnp.where(r >= tq, r - tq, r)
            q_chunk = (q_start + r) >> CHUNK_SHIFT
            k_chunk = (start + lax.broadcasted_iota(I32, (1, tk), 1)) >> CHUNK_SHIFT
            s = jnp.where(k_chunk <= q_chunk, s, -jnp.inf)
        m_prev = m_ref[...]
        m_new = jnp.maximum(m_prev, jnp.max(s, axis=1, keepdims=True))
        alpha = jnp.exp2(m_prev - m_new)
        p = jnp.exp2(s - m_new)
        l_ref[...] = alpha * l_ref[...] + jnp.sum(p, axis=1, keepdims=True)
        acc_ref[...] = alpha * acc_ref[...] + jnp.dot(p.astype(BF16), v_ref[0, pl.ds(start, tk), :],
                                                      preferred_element_type=F32)
        m_ref[...] = m_new

    def full_body(j, c):
        block(j, False)
        return c

    def masked_body(j, c):
        block(j, True)
        return c

    lax.fori_loop(0, n_full, full_body, 0)
    lax.fori_loop(n_full, n_need, masked_body, 0)

    o = acc_ref[...] / l_ref[...]
    if diff:
        lam = (jnp.exp(jnp.sum(lq1_ref[...] * lk1_ref[...], keepdims=True))
               - jnp.exp(jnp.sum(lq2_ref[...] * lk2_ref[...], keepdims=True)) + lam_init)
        od = o[0:tq] - lam * o[tq:]
        o_ref[0] = (_rms(od, gd_ref[...]) * (1.0 - lam_init)).astype(o_ref.dtype)
    else:
        o_ref[0] = o.astype(o_ref.dtype)


def _attention(q, k, v, heads, tq, tk, q_off, diff_args=None, lam_init=0.0):
    b, sq, _ = q.shape
    sk = k.shape[1]
    nq, nk = sq // tq, sk // tk
    diff = diff_args is not None
    kv_spec = pl.BlockSpec((1, sk, LANES), lambda i, h, qi: (i, 0, h))
    in_specs = [pl.BlockSpec((1, tq, LANES), lambda i, h, qi: (i, qi, h)), kv_spec, kv_spec]
    rows = 2 * tq if diff else tq
    scratch = [pltpu.VMEM((rows, 1), F32), pltpu.VMEM((rows, 1), F32), pltpu.VMEM((rows, LANES), F32)]
    args = [q, k, v]
    if diff:
        small = lambda w: pl.BlockSpec((1, w), lambda i, h, qi: (0, 0))
        in_specs += [small(DIFF_HD)] * 4 + [small(2 * DIFF_HD)]
        scratch.append(pltpu.VMEM((rows, LANES), BF16))
        args += list(diff_args)
    kern = functools.partial(_attn_kernel, tq=tq, tk=tk, nk=nk, q_off=q_off, diff=diff, lam_init=lam_init)
    return pl.pallas_call(
        kern,
        grid=(b, heads, nq),
        in_specs=in_specs,
        out_specs=pl.BlockSpec((1, tq, LANES), lambda i, h, qi: (i, qi, h)),
        out_shape=jax.ShapeDtypeStruct((b, sq, heads * LANES), BF16),
        scratch_shapes=scratch,
        compiler_params=_cparams(("arbitrary",) * 3),
    )(*args)


def _out_kernel(x_ref, om_ref, od_ref, wout_ref, mod_ref, gf_ref, wr_ref, br_ref,
                x1_ref, hp_ref, idx_ref, gate_ref):
    a = (jnp.dot(om_ref[0], wout_ref[0:MLA_PAD, :], preferred_element_type=F32)
         + jnp.dot(od_ref[0], wout_ref[MLA_PAD:, :], preferred_element_type=F32))
    mod = mod_ref[0]
    x1 = x_ref[0] + mod[2:3] * a
    x1_ref[0] = x1
    h2 = _rms(x1, gf_ref[...]) * (1.0 + mod[4:5]) + mod[3:4]
    bits = lax.bitcast_convert_type(h2.astype(BF16).astype(F32), U32)
    half = h2.shape[1] // 2
    hp_ref[0] = (bits[:, :half] >> 16) | (bits[:, half:] & jnp.uint32(0xFFFF0000))

    logits = lax.dot_general(wr_ref[...], h2, (((1,), (1,)), ((), ())), precision=lax.Precision.HIGHEST,
                             preferred_element_type=F32) + br_ref[...]
    e_iota = lax.broadcasted_iota(I32, logits.shape, 0).astype(F32)
    vals, idxs = [], []
    for _ in range(TOP_K):
        m = jnp.max(logits, axis=0, keepdims=True)
        i = jnp.min(jnp.where(logits == m, e_iota, float(N_EXPERTS)), axis=0, keepdims=True)
        vals.append(m)
        idxs.append(i)
        logits = jnp.where(e_iota == i, -jnp.inf, logits)
    ex = [jnp.exp(v - vals[0]) for v in vals]
    tot = ex[0] + ex[1] + ex[2] + ex[3]
    idx_ref[0] = jnp.concatenate(idxs, axis=0).astype(I32)
    gate_ref[0] = jnp.concatenate([e / tot for e in ex], axis=0)


def _out(x, om, od, w_out_p, mod, g_ffn, w_r, b_r, t):
    b, s, d = x.shape
    ns = s // t
    full = lambda shape: pl.BlockSpec(shape, lambda i, j: (0,) * len(shape))
    tok = lambda w: pl.BlockSpec((1, t, w), lambda i, j: (i, j, 0))
    flat = pl.BlockSpec((1, TOP_K, t), lambda i, j: (i * ns + j, 0, 0))
    x1, hp, idx, gates = pl.pallas_call(
        _out_kernel,
        grid=(b, ns),
        in_specs=[tok(d), tok(MLA_PAD), tok(DIFF_QKV), full((MLA_PAD + DIFF_QKV, d)),
                  pl.BlockSpec((1, 6, d), lambda i, j: (i, 0, 0)), full((1, d)),
                  full((N_EXPERTS, d)), full((N_EXPERTS, 1))],
        out_specs=[tok(d), tok(d // 2), flat, flat],
        out_shape=[jax.ShapeDtypeStruct((b, s, d), F32), jax.ShapeDtypeStruct((b, s, d // 2), U32),
                   jax.ShapeDtypeStruct((b * ns, TOP_K, t), I32), jax.ShapeDtypeStruct((b * ns, TOP_K, t), F32)],
        compiler_params=_cparams(("arbitrary", "arbitrary")),
    )(x, om, od, w_out_p, mod, g_ffn, w_r, b_r)
    unflat = lambda a: a.transpose(1, 0, 2).reshape(TOP_K, b * s)
    return x1, hp, unflat(idx), unflat(gates)


def _rank_kernel(idx_ref, rank_ref, cnt_ref, carry_ref):
    step = pl.program_id(0)

    @pl.when(step == 0)
    def _():
        carry_ref[...] = jnp.zeros(carry_ref.shape, F32)

    idx = idx_ref[...]
    t = idx.shape[1]
    e_iota = lax.broadcasted_iota(I32, (N_EXPERTS, t), 0)
    hits = [e_iota == idx[k:k + 1, :] for k in range(TOP_K)]
    onehot = sum(h.astype(F32) for h in hits)
    earlier = (lax.broadcasted_iota(I32, (t, t), 0) < lax.broadcasted_iota(I32, (t, t), 1)).astype(BF16)
    before = jnp.dot(onehot.astype(BF16), earlier, preferred_element_type=F32) + carry_ref[...]
    rank_ref[...] = jnp.concatenate(
        [jnp.sum(jnp.where(h, before, 0.0), axis=0, keepdims=True) for h in hits], axis=0).astype(I32)
    carry_ref[...] = carry_ref[...] + jnp.sum(onehot, axis=1, keepdims=True)
    cnt_ref[...] = carry_ref[...].astype(I32)


def _rank(idx, t):
    n = idx.shape[1]
    return pl.pallas_call(
        _rank_kernel,
        grid=(n // t,),
        in_specs=[pl.BlockSpec((TOP_K, t), lambda i: (0, i))],
        out_specs=[pl.BlockSpec((TOP_K, t), lambda i: (0, i)), pl.BlockSpec((N_EXPERTS, 1), lambda i: (0, 0))],
        out_shape=[jax.ShapeDtypeStruct((TOP_K, n), I32), jax.ShapeDtypeStruct((N_EXPERTS, 1), I32)],
        scratch_shapes=[pltpu.VMEM((N_EXPERTS, 1), F32)],
        compiler_params=_cparams(("arbitrary",)),
    )(idx)


def _dispatch_kernel(dest_ref, dnxt_ref, hp_ref, xs_in_ref, xs_ref, dsm, sem, idx_sem, *, t, n_steps):
    del xs_in_ref
    i = pl.program_id(0)
    slot = i % 2

    @pl.when(i == 0)
    def _():
        cp = pltpu.make_async_copy(dest_ref.at[0], dsm.at[0], idx_sem)
        cp.start()
        cp.wait()

    nxt = pltpu.make_async_copy(dnxt_ref.at[0], dsm.at[1 - slot], idx_sem)

    @pl.when(i + 1 < n_steps)
    def _():
        nxt.start()

    def issue(tok, c):
        for k in range(TOP_K):
            pltpu.make_async_copy(hp_ref.at[pl.ds(tok, 1)], xs_ref.at[pl.ds(dsm[slot, k, tok], 1)], sem).start()
        return c

    lax.fori_loop(0, t, issue, 0, unroll=ISSUE_UNROLL)
    for k in range(TOP_K):
        pltpu.make_async_copy(hp_ref, xs_ref.at[pl.ds(0, t)], sem).wait()

    @pl.when(i + 1 < n_steps)
    def _():
        nxt.wait()


def _dispatch(dest3, hp, n_slots, t):
    n, w = hp.shape
    n_steps = n // t
    zeros = jnp.zeros((n_slots, w), U32)
    return pl.pallas_call(
        functools.partial(_dispatch_kernel, t=t, n_steps=n_steps),
        grid=(n_steps,),
        in_specs=[pl.BlockSpec((1, TOP_K, t), lambda i: (i, 0, 0)),
                  pl.BlockSpec((1, TOP_K, t), lambda i: (jnp.minimum(i + 1, n_steps - 1), 0, 0)),
                  pl.BlockSpec((t, w), lambda i: (i, 0)),
                  pl.BlockSpec(memory_space=pl.ANY)],
        out_specs=pl.BlockSpec(memory_space=pl.ANY),
        out_shape=jax.ShapeDtypeStruct((n_slots, w), U32),
        scratch_shapes=[pltpu.SMEM((2, TOP_K, t), I32), pltpu.SemaphoreType.DMA, pltpu.SemaphoreType.DMA],
        input_output_aliases={3: 0},
        compiler_params=_cparams(("arbitrary",)),
    )(dest3, dest3, hp, zeros)


def _expert_kernel(te_ref, nu_ref, xs_ref, wgu_ref, bgu_ref, wd_ref, bd_ref, ys_ref):
    del te_ref

    @pl.when(pl.program_id(0) < nu_ref[0])
    def _():
        p = xs_ref[...]
        x_lo = lax.bitcast_convert_type(p << 16, F32).astype(BF16)
        x_hi = lax.bitcast_convert_type(p & jnp.uint32(0xFFFF0000), F32).astype(BF16)
        half = p.shape[1]
        gu = (jnp.dot(x_lo, wgu_ref[0, 0:half, :], preferred_element_type=F32)
              + jnp.dot(x_hi, wgu_ref[0, half:, :], preferred_element_type=F32) + bgu_ref[0])
        f = gu.shape[1] // 2
        glu = jnp.minimum(gu[:, :f], SWIGLU_LIMIT)
        lin = jnp.clip(gu[:, f:], -SWIGLU_LIMIT, SWIGLU_LIMIT)
        act = glu * jax.nn.sigmoid(SWIGLU_ALPHA * glu) * (lin + 1.0)
        ys_ref[...] = jnp.dot(act.astype(BF16), wd_ref[0], preferred_element_type=F32) + bd_ref[0]


def _experts(tile_e, n_used, xs, w_gu, b_gu, w_down, b_down, tm):
    n_slots, half = xs.shape
    e, d, f2 = w_gu.shape
    n_tiles = n_slots // tm
    row = lambda i, te, nu: (jnp.minimum(i, nu[0] - 1), 0)
    wsel = lambda i, te, nu: (te[i], 0, 0)
    grid_spec = pltpu.PrefetchScalarGridSpec(
        num_scalar_prefetch=2,
        grid=(n_tiles,),
        in_specs=[pl.BlockSpec((tm, half), row),
                  pl.BlockSpec((1, d, f2), wsel), pl.BlockSpec((1, 1, f2), wsel),
                  pl.BlockSpec((1, f2 // 2, d), wsel), pl.BlockSpec((1, 1, d), wsel)],
        out_specs=pl.BlockSpec((tm, d), row),
    )
    return pl.pallas_call(
        _expert_kernel,
        grid_spec=grid_spec,
        out_shape=jax.ShapeDtypeStruct((n_slots, d), F32),
        compiler_params=_cparams(("arbitrary",)),
    )(tile_e, n_used, xs, w_gu, b_gu.reshape(e, 1, f2), w_down, b_down.reshape(e, 1, d))


def _combine_kernel(dcur_ref, dnxt_ref, dnn_ref, gate_ref, x1_ref, mod_ref, gfin_ref, ys_ref,
                    x2_ref, buf, dsm, sem, idx_sem, *, t, n_steps, final):
    i = pl.program_id(0)
    slot = i % 2

    def load_idx(src, s):
        cp = pltpu.make_async_copy(src.at[0], dsm.at[s], idx_sem)
        cp.start()
        cp.wait()

    def issue(s, si):
        def body(tok, c):
            for k in range(TOP_K):
                pltpu.make_async_copy(ys_ref.at[pl.ds(dsm[si, k, tok], 1)], buf.at[s, k, pl.ds(tok, 1)],
                                      sem.at[s]).start()
            return c
        lax.fori_loop(0, t, body, 0, unroll=ISSUE_UNROLL)

    @pl.when(i == 0)
    def _():
        load_idx(dcur_ref, 0)
        issue(0, 0)
        if n_steps > 1:
            load_idx(dnxt_ref, 1)

    ahead = pltpu.make_async_copy(dnn_ref.at[0], dsm.at[(i + 2) % 3], idx_sem)

    @pl.when(i + 2 < n_steps)
    def _():
        ahead.start()

    @pl.when(i + 1 < n_steps)
    def _():
        issue(1 - slot, (i + 1) % 3)

    for k in range(TOP_K):
        pltpu.make_async_copy(ys_ref.at[pl.ds(0, t)], buf.at[slot, k], sem.at[slot]).wait()

    g = gate_ref[...]
    y = g[:, 0:1] * buf[slot, 0]
    for k in range(1, TOP_K):
        y = y + g[:, k:k + 1] * buf[slot, k]
    x2 = x1_ref[...] + mod_ref[0][5:6] * y
    x2_ref[...] = _rms(x2, gfin_ref[...]) if final else x2

    @pl.when(i + 2 < n_steps)
    def _():
        ahead.wait()


def _combine(dest3, gates_t, x1, mod, g_final, ys, s, t, final):
    n, d = x1.shape
    n_steps = n // t
    per_b = s // t
    return pl.pallas_call(
        functools.partial(_combine_kernel, t=t, n_steps=n_steps, final=final),
        grid=(n_steps,),
        in_specs=[pl.BlockSpec((1, TOP_K, t), lambda i: (i, 0, 0)),
                  pl.BlockSpec((1, TOP_K, t), lambda i: (jnp.minimum(i + 1, n_steps - 1), 0, 0)),
                  pl.BlockSpec((1, TOP_K, t), lambda i: (jnp.minimum(i + 2, n_steps - 1), 0, 0)),
                  pl.BlockSpec((t, TOP_K), lambda i: (i, 0)),
                  pl.BlockSpec((t, d), lambda i: (i, 0)),
                  pl.BlockSpec((1, 6, d), lambda i: (i // per_b, 0, 0)),
                  pl.BlockSpec((1, d), lambda i: (0, 0)),
                  pl.BlockSpec(memory_space=pl.ANY)],
        out_specs=pl.BlockSpec((t, d), lambda i: (i, 0)),
        out_shape=jax.ShapeDtypeStruct((n, d), F32),
        scratch_shapes=[pltpu.VMEM((2, TOP_K, t, d), F32), pltpu.SMEM((3, TOP_K, t), I32),
                        pltpu.SemaphoreType.DMA((2,)), pltpu.SemaphoreType.DMA],
        compiler_params=_cparams(("arbitrary",)),
    )(dest3, dest3, dest3, gates_t, x1, mod, g_final, ys)


def _rope_tables(pos, d, lane_pos):
    inv = ROPE_THETA ** (-jnp.arange(0, d, 2, dtype=F32) / d)
    ang = pos.astype(F32)[:, None] * inv[None, :]
    cos, sin = jnp.cos(ang), jnp.sin(ang)
    cos_t = jnp.ones((pos.shape[0], LANES), F32)
    sin_t = jnp.zeros((pos.shape[0], LANES), F32)
    for o in lane_pos:
        cos_t = cos_t.at[:, o:o + d].set(jnp.concatenate([cos, cos], axis=1))
        sin_t = sin_t.at[:, o:o + d].set(jnp.concatenate([-sin, sin], axis=1))
    return cos_t, sin_t


def _prep_weights(w_in, w_uq, w_uk, w_uv, w_out):
    depth, d, _ = w_in.shape
    cuts = [MLA_Q_LORA, MLA_KV_LORA, MLA_ROPE, DIFF_QKV, DIFF_QKV, DIFF_QKV]
    offs = [0]
    for c in cuts:
        offs.append(offs[-1] + c)
    parts = [w_in[:, :, offs[i]:offs[i + 1]] for i in range(6)]
    kr_pad = jnp.pad(parts[2], ((0, 0), (0, 0), (MLA_NOPE, LANES - MLA_NOPE - MLA_ROPE)))
    w_in_p = jnp.concatenate([parts[0], parts[1], kr_pad, parts[3], parts[4], parts[5]], axis=2).astype(BF16)
    qd = MLA_NOPE + MLA_ROPE
    w_uq_p = jnp.pad(w_uq.reshape(depth, MLA_Q_LORA, MLA_HEADS, qd), ((0, 0), (0, 0), (0, 0), (0, LANES - qd)))
    w_uq_p = w_uq_p.reshape(depth, MLA_Q_LORA, MLA_PAD).astype(BF16)
    pad_head = lambda w, n: jnp.pad(w, ((0, 0), (0, 0), (0, 0), (0, LANES - n))).reshape(depth, MLA_KV_LORA, MLA_PAD)
    w_ukv_p = jnp.concatenate([pad_head(w_uk, MLA_NOPE), pad_head(w_uv, MLA_V)], axis=2).astype(BF16)
    wo_m = jnp.pad(w_out[:, :MLA_HEADS * MLA_V].reshape(depth, MLA_HEADS, MLA_V, d),
                   ((0, 0), (0, 0), (0, LANES - MLA_V), (0, 0))).reshape(depth, MLA_PAD, d)
    w_out_p = jnp.concatenate([wo_m, w_out[:, MLA_HEADS * MLA_V:]], axis=1).astype(BF16)
    place = jnp.zeros((MLA_ROPE, 2 * MLA_PAD), F32)
    eye = jnp.eye(MLA_ROPE, dtype=F32)
    for hd in range(MLA_HEADS):
        place = place.at[:, hd * LANES + MLA_NOPE:hd * LANES + MLA_NOPE + MLA_ROPE].set(eye)
    w_uvt_p = jnp.swapaxes(w_ukv_p[:, :, MLA_PAD:], 1, 2)
    return w_in_p, w_uq_p, w_ukv_p, w_out_p, place.astype(BF16), w_uvt_p


def _pick(n, pref):
    t = min(n, pref)
    while n % t:
        t //= 2
    return t


def _moe(hp, idx, gates, x1, mod_l, g_final, wts, s, final):
    w_gu, b_gu, w_down, b_down = wts
    n = hp.shape[0]
    rank, counts = _rank(idx, _pick(n, RANK_TILE))
    counts = counts[:, 0]
    tm = MOE_TILE if n * TOP_K >= N_EXPERTS * MOE_TILE else MOE_TILE_SMALL
    padded = (counts + tm - 1) // tm * tm
    pad_ends = jnp.cumsum(padded)
    pad_starts = pad_ends - padded
    dest = rank + jnp.sum(jnp.where(idx[None] == jnp.arange(N_EXPERTS, dtype=I32)[:, None, None],
                                    pad_starts[:, None, None], 0), axis=0)
    n_tiles = -(-(n * TOP_K) // tm) + N_EXPERTS
    tile_e = jnp.minimum(jnp.sum(pad_ends[None, :] <= (jnp.arange(n_tiles, dtype=I32) * tm)[:, None], axis=1),
                         N_EXPERTS - 1).astype(I32)
    n_used = (pad_ends[-1:] // tm).astype(I32)
    td = _pick(n, DISPATCH_TILE)
    xs = _dispatch(dest.reshape(TOP_K, n // td, td).transpose(1, 0, 2), hp, n_tiles * tm, td)
    ys = _experts(tile_e, n_used, xs, w_gu, b_gu, w_down, b_down, tm)
    tc = _pick(s, COMBINE_TILE)
    return _combine(dest.reshape(TOP_K, n // tc, tc).transpose(1, 0, 2), gates.T, x1, mod_l, g_final, ys, s, tc,
                    final)


def _trunk(x, c, caches, prm, prepped, q_off):
    b, s, d = x.shape
    depth = prm['w_in'].shape[0]
    w_in_p, w_uq_p, w_ukv_p, w_out_p, place, w_uvt_p, w_ada_b, w_gu_b, w_down_b = prepped
    rows = -(-b // 16) * 16
    mod = _ada(jnp.pad(c, ((0, rows - b), (0, 0))), w_ada_b, prm['b_ada'])[:, :b].reshape(depth, b, 6, d)
    pos = q_off + jnp.arange(s, dtype=I32)
    cos_q, sin_q = _rope_tables(pos, MLA_ROPE, (MLA_NOPE,))
    cos_d, sin_d = _rope_tables(pos, DIFF_HD, (0, DIFF_HD))
    t = _pick(s, TOK_TILE)
    stacks = None
    row = lambda v: v.reshape(1, -1)
    for l in range(depth):
        qm, qd, stacks = _proj(x, mod[l], row(prm['g_attn'][l]), w_in_p[l], row(prm['g_q'][l]), w_uq_p[l],
                               row(prm['g_kv'][l]), (cos_q, sin_q, cos_d, sin_d), t, l, depth, stacks)
        lam_init = 0.8 - 0.6 * math.exp(-0.3 * l)
        diff_args = (row(prm['lam_q1'][l]), row(prm['lam_k1'][l]), row(prm['lam_q2'][l]),
                     row(prm['lam_k2'][l]), row(prm['g_diff'][l]))
        if caches is None and q_off == 0 and s % ATTN_TK == 0:
            km, vm_t, dkb, dv_t = _kvt(stacks, l, w_ukv_p[l], place, w_uvt_p[l], ATTN_TK)
            om = _attention_t(qm, km, vm_t, MLA_HEADS, _pick(s, ATTN_TQ))
            od = _attention_t(qd, dkb, dv_t, DIFF_HEADS, _pick(s, ATTN_TQ // 2), diff_args, lam_init)
        else:
            if caches is not None:
                lat_a, kr_a, dk_a, dv_a = (jnp.concatenate([cc[l].reshape(b, cc.shape[2], -1), st[l]], axis=1)
                                           for cc, st in zip(caches, stacks))
            else:
                lat_a, kr_a, dk_a, dv_a = (st[l] for st in stacks)
            sk = lat_a.shape[1]
            tk = ATTN_TK if sk % ATTN_TK == 0 else sk
            tkv = max(t_ for t_ in range(16, min(sk, 2 * TOK_TILE) + 1, 16) if sk % t_ == 0)
            km, vm, dkb, dvb = _kv(lat_a, kr_a, dk_a, dv_a, w_ukv_p[l], place, tkv)
            om = _attention(qm, km, vm, MLA_HEADS, _pick(s, ATTN_TQ), tk, q_off)
            od = _attention(qd, dkb, dvb, DIFF_HEADS, _pick(s, ATTN_TQ // 2), tk, q_off, diff_args, lam_init)
        x1, hp, idx, gates = _out(x, om, od, w_out_p[l], mod[l], row(prm['g_ffn'][l]),
                                  prm['w_router'][l].T, prm['b_router'][l].reshape(-1, 1), t)
        x = _moe(hp.reshape(b * s, d // 2), idx, gates, x1.reshape(b * s, d), mod[l], row(prm['g_final']),
                 (w_gu_b[l], prm['b_gate_up'][l], w_down_b[l], prm['b_down'][l]), s, l == depth - 1).reshape(b, s, d)
    lat_n, kr_n, dk_n, dv_n = stacks
    shp = (depth, b, s, DIFF_HEADS, 2 * DIFF_HD)
    return x, (lat_n, kr_n, dk_n.reshape(shp), dv_n.reshape(shp))


def kernel(x_prompt, x_sample, c_prompt, c_sample, cache_mla_latent, cache_mla_krope, cache_diff_k, cache_diff_v,
           w_ada, b_ada, g_attn, g_ffn, w_in, g_q, w_uq, g_kv, w_uk, w_uv, lam_q1, lam_k1, lam_q2, lam_k2, g_diff,
           w_out, w_router, b_router, w_gate_up, b_gate_up, w_down, b_down, g_final):
    prm = {'b_ada': b_ada, 'g_attn': g_attn, 'g_ffn': g_ffn, 'w_in': w_in, 'g_q': g_q, 'g_kv': g_kv,
           'lam_q1': lam_q1, 'lam_k1': lam_k1, 'lam_q2': lam_q2, 'lam_k2': lam_k2, 'g_diff': g_diff,
           'w_router': w_router, 'b_router': b_router, 'b_gate_up': b_gate_up, 'b_down': b_down,
           'g_final': g_final}
    prepped = _prep_weights(w_in, w_uq, w_uk, w_uv, w_out) + (
        w_ada.astype(BF16), w_gate_up.astype(BF16), w_down.astype(BF16))
    y_p, rows_p = _trunk(x_prompt, c_prompt, None, prm, prepped, 0)
    caches = (cache_mla_latent, cache_mla_krope, cache_diff_k, cache_diff_v)
    y_s, rows_s = _trunk(x_sample, c_sample, caches, prm, prepped, cache_mla_latent.shape[2])
    return (y_p, y_s) + rows_p + rows_s
```

```python
import functools
import math

import jax
import jax.numpy as jnp
from jax import lax
from jax.experimental import pallas as pl
from jax.experimental.pallas import tpu as pltpu

F32 = jnp.float32
BF16 = jnp.bfloat16
I32 = jnp.int32
U32 = jnp.uint32

CHUNK = 64
ROPE_THETA = 10000.0
EPS = 1e-6
MLA_HEADS = 8
MLA_Q_LORA = 256
MLA_KV_LORA = 128
MLA_NOPE = 64
MLA_ROPE = 32
MLA_V = 64
DIFF_HEADS = 4
DIFF_HD = 64
DIFF_QKV = DIFF_HEADS * 2 * DIFF_HD
N_EXPERTS = 32
TOP_K = 4
SWIGLU_LIMIT = 7.0
SWIGLU_ALPHA = 1.702

LANES = 128
MLA_PAD = MLA_HEADS * LANES
PROJ_COLS = MLA_Q_LORA + MLA_KV_LORA + LANES + 3 * DIFF_QKV
VMEM_LIMIT = 56 * 1024 * 1024
MOE_TILE = 512
MOE_TILE_SMALL = 128
TOK_TILE = 512
ATTN_TQ = 1024
ATTN_TK = 512
ATTN_GROUP = 4
RANK_TILE = 512
DISPATCH_TILE = 512
COMBINE_TILE = 256
ISSUE_UNROLL = 4
NEG_BIG = -1e30
LOG2E = math.log2(math.e)
assert CHUNK & (CHUNK - 1) == 0
CHUNK_SHIFT = CHUNK.bit_length() - 1


def _cparams(sem):
    return pltpu.CompilerParams(dimension_semantics=sem, vmem_limit_bytes=VMEM_LIMIT)


def _rms(x, g):
    return x * lax.rsqrt(jnp.mean(x * x, axis=-1, keepdims=True) + EPS) * g


def _rope_lanes(v, cos, sin_signed, half):
    lane = lax.broadcasted_iota(I32, v.shape, 1)
    first = (lane % (2 * half)) < half
    swapped = jnp.where(first, pltpu.roll(v, LANES - half, 1), pltpu.roll(v, half, 1))
    return v * cos + swapped * sin_signed


def _ada_kernel(c_ref, w_ref, b_ref, o_ref):
    c = c_ref[...]
    s = c * jax.nn.sigmoid(c)
    o_ref[0] = jnp.dot(s.astype(BF16), w_ref[0], preferred_element_type=F32) + b_ref[0]


def _ada(c, w_ada, b_ada):
    depth, d, n6 = w_ada.shape
    rows = c.shape[0]
    tn = 1536
    return pl.pallas_call(
        _ada_kernel,
        grid=(depth, n6 // tn),
        in_specs=[pl.BlockSpec((rows, d), lambda l, j: (0, 0)),
                  pl.BlockSpec((1, d, tn), lambda l, j: (l, 0, j)),
                  pl.BlockSpec((1, 1, tn), lambda l, j: (l, 0, j))],
        out_specs=pl.BlockSpec((1, rows, tn), lambda l, j: (l, 0, j)),
        out_shape=jax.ShapeDtypeStruct((depth, rows, n6), F32),
        compiler_params=_cparams(("arbitrary", "arbitrary")),
    )(c, w_ada, b_ada.reshape(depth, 1, n6))


def _proj_kernel(x_ref, mod_ref, ga_ref, win_ref, gq_ref, wuq_ref, gkv_ref,
                 cq_ref, sq_ref, cd_ref, sd_ref, *rest):
    qm_ref, qd_ref, lat_ref, kr_ref, dk_ref, dv_ref = rest[-6:]
    x = x_ref[0]
    mod = mod_ref[0]
    h = _rms(x, ga_ref[...]) * (1.0 + mod[1:2]) + mod[0:1]
    proj = jnp.dot(h.astype(BF16), win_ref[...], preferred_element_type=F32)
    o = 0
    cq = proj[:, o:o + MLA_Q_LORA]; o += MLA_Q_LORA
    ckv = proj[:, o:o + MLA_KV_LORA]; o += MLA_KV_LORA
    krp = proj[:, o:o + LANES]; o += LANES
    dq = proj[:, o:o + DIFF_QKV]; o += DIFF_QKV
    dk = proj[:, o:o + DIFF_QKV]; o += DIFF_QKV
    dv = proj[:, o:o + DIFF_QKV]

    cos_q, sin_q = cq_ref[...], sq_ref[...]
    cos_d, sin_d = cd_ref[...], sd_ref[...]
    q = jnp.dot(_rms(cq, gq_ref[...]).astype(BF16), wuq_ref[...], preferred_element_type=F32)
    q_scale = LOG2E / math.sqrt(MLA_NOPE + MLA_ROPE)
    for hd in range(MLA_HEADS):
        sl = slice(hd * LANES, (hd + 1) * LANES)
        qm_ref[0, :, sl] = (_rope_lanes(q[:, sl], cos_q, sin_q, MLA_ROPE // 2) * q_scale).astype(BF16)
    lat_ref[0, 0] = _rms(ckv, gkv_ref[...])
    kr_ref[0, 0] = _rope_lanes(krp, cos_q, sin_q, MLA_ROPE // 2)[:, MLA_NOPE:MLA_NOPE + MLA_ROPE]
    d_scale = LOG2E / math.sqrt(DIFF_HD)
    for hd in range(DIFF_HEADS):
        sl = slice(hd * LANES, (hd + 1) * LANES)
        qd_ref[0, :, sl] = (_rope_lanes(dq[:, sl], cos_d, sin_d, DIFF_HD // 2) * d_scale).astype(BF16)
        dk_ref[0, 0, :, sl] = _rope_lanes(dk[:, sl], cos_d, sin_d, DIFF_HD // 2)
    dv_ref[0, 0] = dv


ROW_WIDTHS = (MLA_KV_LORA, MLA_ROPE, DIFF_QKV, DIFF_QKV)


def _proj(x, mod, g_attn, w_in_p, g_q, w_uq_p, g_kv, tabs, t, layer, depth, stacks):
    b, s, d = x.shape
    cos_q, sin_q, cos_d, sin_d = tabs
    full = lambda shape: pl.BlockSpec(shape, lambda i, j: (0,) * len(shape))
    tok = lambda w: pl.BlockSpec((1, t, w), lambda i, j: (i, j, 0))
    tab = pl.BlockSpec((t, LANES), lambda i, j: (j, 0))
    lay = lambda w: pl.BlockSpec((1, 1, t, w), lambda i, j: (layer, i, j, 0))
    in_specs = [tok(d), pl.BlockSpec((1, 6, d), lambda i, j: (i, 0, 0)), full((1, d)),
                full((d, PROJ_COLS)), full((1, MLA_Q_LORA)), full((MLA_Q_LORA, MLA_PAD)),
                full((1, MLA_KV_LORA)), tab, tab, tab, tab]
    args = [x, mod, g_attn, w_in_p, g_q, w_uq_p, g_kv, cos_q, sin_q, cos_d, sin_d]
    aliases = {len(args) + k: 2 + k for k in range(len(ROW_WIDTHS))}
    in_specs += [pl.BlockSpec(memory_space=pl.ANY)] * len(ROW_WIDTHS)
    args += list(stacks)
    out = pl.pallas_call(
        _proj_kernel,
        grid=(b, s // t),
        in_specs=in_specs,
        out_specs=[tok(MLA_PAD), tok(DIFF_QKV)] + [lay(w) for w in ROW_WIDTHS],
        out_shape=[jax.ShapeDtypeStruct((b, s, MLA_PAD), BF16), jax.ShapeDtypeStruct((b, s, DIFF_QKV), BF16)]
        + [jax.ShapeDtypeStruct((depth, b, s, w), F32) for w in ROW_WIDTHS],
        input_output_aliases=aliases,
        compiler_params=_cparams(("arbitrary", "arbitrary")),
    )(*args)
    return out[0], out[1], tuple(out[2:])


def _kv_kernel(lat_ref, kr_ref, dk_ref, dv_ref, wukv_ref, place_ref, km_ref, vm_ref, dkb_ref, dvb_ref):
    kv = jnp.dot(lat_ref[0].astype(BF16), wukv_ref[...], preferred_element_type=F32)
    kv = kv + jnp.dot(kr_ref[0].astype(BF16), place_ref[...], preferred_element_type=F32)
    km_ref[0] = kv[:, :MLA_PAD].astype(BF16)
    vm_ref[0] = kv[:, MLA_PAD:].astype(BF16)
    dkb_ref[0] = dk_ref[0].astype(BF16)
    dvb_ref[0] = dv_ref[0].astype(BF16)


def _kv(lat, kr, dk, dv, w_ukv_p, place, t):
    b, s, _ = lat.shape
    full = lambda shape: pl.BlockSpec(shape, lambda i, j: (0,) * len(shape))
    tok = lambda w: pl.BlockSpec((1, t, w), lambda i, j: (i, j, 0))
    return pl.pallas_call(
        _kv_kernel,
        grid=(b, s // t),
        in_specs=[tok(MLA_KV_LORA), tok(MLA_ROPE), tok(DIFF_QKV), tok(DIFF_QKV),
                  full((MLA_KV_LORA, 2 * MLA_PAD)), full((MLA_ROPE, 2 * MLA_PAD))],
        out_specs=[tok(MLA_PAD), tok(MLA_PAD), tok(DIFF_QKV), tok(DIFF_QKV)],
        out_shape=[jax.ShapeDtypeStruct((b, s, MLA_PAD), BF16), jax.ShapeDtypeStruct((b, s, MLA_PAD), BF16),
                   jax.ShapeDtypeStruct((b, s, DIFF_QKV), BF16), jax.ShapeDtypeStruct((b, s, DIFF_QKV), BF16)],
        compiler_params=_cparams(("arbitrary", "arbitrary")),
    )(lat, kr, dk, dv, w_ukv_p, place)


def _kvt_kernel(lat_ref, kr_ref, dk_ref, dv_ref, wukv_ref, place_ref, wuvt_ref, eye_ref,
                km_ref, vt_ref, dkb_ref, dvt_ref):
    lat = lat_ref[0, 0].astype(BF16)
    k = (jnp.dot(lat, wukv_ref[:, :MLA_PAD], preferred_element_type=F32)
         + jnp.dot(kr_ref[0, 0].astype(BF16), place_ref[:, :MLA_PAD], preferred_element_type=F32))
    km_ref[0] = k.astype(BF16)
    nt = (((1,), (1,)), ((), ()))
    v_t = lax.dot_general(wuvt_ref[...], lat, nt, preferred_element_type=F32)
    for hd in range(MLA_HEADS):
        vt_ref[0, hd, 0] = v_t[hd * LANES:(hd + 1) * LANES, :].astype(BF16)
    dkb_ref[0] = dk_ref[0, 0].astype(BF16)
    dv_t = lax.dot_general(eye_ref[...], dv_ref[0, 0].astype(BF16), nt, preferred_element_type=F32)
    for hd in range(DIFF_HEADS):
        dvt_ref[0, hd, 0] = dv_t[hd * LANES:(hd + 1) * LANES, :].astype(BF16)


def _kvt(stacks, layer, w_ukv_p, place, w_uvt_p, t):
    _, b, s, _ = stacks[0].shape
    nk = s // t
    full = lambda shape: pl.BlockSpec(shape, lambda i, j: (0,) * len(shape))
    tok = lambda w: pl.BlockSpec((1, t, w), lambda i, j: (i, j, 0))
    lay = lambda w: pl.BlockSpec((1, 1, t, w), lambda i, j: (layer, i, j, 0))
    tr = lambda h: pl.BlockSpec((1, h, 1, LANES, t), lambda i, j: (i, 0, j, 0, 0))
    eye = jnp.eye(DIFF_QKV, dtype=BF16)
    lat, kr, dk, dv = stacks
    return pl.pallas_call(
        _kvt_kernel,
        grid=(b, nk),
        in_specs=[lay(w) for w in ROW_WIDTHS]
        + [full((MLA_KV_LORA, 2 * MLA_PAD)), full((MLA_ROPE, 2 * MLA_PAD)),
           full((MLA_PAD, MLA_KV_LORA)), full((DIFF_QKV, DIFF_QKV))],
        out_specs=[tok(MLA_PAD), tr(MLA_HEADS), tok(DIFF_QKV), tr(DIFF_HEADS)],
        out_shape=[jax.ShapeDtypeStruct((b, s, MLA_PAD), BF16),
                   jax.ShapeDtypeStruct((b, MLA_HEADS, nk, LANES, t), BF16),
                   jax.ShapeDtypeStruct((b, s, DIFF_QKV), BF16),
                   jax.ShapeDtypeStruct((b, DIFF_HEADS, nk, LANES, t), BF16)],
        compiler_params=_cparams(("arbitrary", "arbitrary")),
    )(lat, kr, dk, dv, w_ukv_p, place, w_uvt_p, eye)


def _attn_t_kernel(*refs, tq, tk, nk, diff, lam_init):
    if diff:
        (q_ref, k_ref, vt_ref, lq1_ref, lk1_ref, lq2_ref, lk2_ref, gd_ref,
         o_ref, m_ref, l_ref, acc_ref, sa_ref, sb_ref, qs_ref) = refs
    else:
        q_ref, k_ref, vt_ref, o_ref, m_ref, l_ref, acc_ref, sa_ref, sb_ref = refs
    qi = pl.program_id(2)
    cols = 2 * tq if diff else tq

    m_ref[...] = jnp.full(m_ref.shape, NEG_BIG, F32)
    l_ref[...] = jnp.zeros(l_ref.shape, F32)
    acc_ref[...] = jnp.zeros(acc_ref.shape, F32)
    if diff:
        q = q_ref[0]
        lane = lax.broadcasted_iota(I32, q.shape, 1)
        qs_ref[0:tq, :] = jnp.where(lane < DIFF_HD, q, jnp.zeros_like(q))
        qs_ref[tq:, :] = jnp.where(lane >= DIFF_HD, q, jnp.zeros_like(q))

    q_start = qi * tq
    seen_by_all = ((q_start >> CHUNK_SHIFT) + 1) << CHUNK_SHIFT
    seen_by_any = (((q_start + tq - 1) >> CHUNK_SHIFT) + 1) << CHUNK_SHIFT
    n_full = jnp.minimum(seen_by_all // tk, nk)
    n_need = jnp.minimum((seen_by_any + tk - 1) // tk, nk)

    def scores(j, c0=0):
        start = pl.multiple_of(jnp.minimum(j, nk - 1) * tk, tk)
        q = qs_ref[c0:, :] if diff else q_ref[0, c0:, :]
        return lax.dot_general(k_ref[0, pl.ds(start, tk), :], q, (((1,), (1,)), ((), ())),
                               preferred_element_type=F32)

    def absorb(s_ref, j, masked, c0=0):
        s = s_ref[:, c0:]
        if masked:
            c = c0 + lax.broadcasted_iota(I32, (1, cols - c0), 1)
            if diff:
                c = jnp.where(c >= tq, c - tq, c)
            q_chunk = (q_start + c) >> CHUNK_SHIFT
            k_chunk = (j * tk + lax.broadcasted_iota(I32, (tk, 1), 0)) >> CHUNK_SHIFT
            s = jnp.where(k_chunk <= q_chunk, s, -jnp.inf)
        m_prev = m_ref[:, c0:]
        m_new = jnp.maximum(m_prev, jnp.max(s, axis=0, keepdims=True))
        alpha = jnp.exp2(m_prev - m_new)
        p = jnp.exp2(s - m_new)
        l_ref[:, c0:] = alpha * l_ref[:, c0:] + jnp.sum(p, axis=0, keepdims=True)
        acc_ref[:, c0:] = alpha * acc_ref[:, c0:] + jnp.dot(vt_ref[0, 0, j], p.astype(BF16),
                                                            preferred_element_type=F32)
        m_ref[:, c0:] = m_new

    sa_ref[...] = scores(0)
    bufs = (sa_ref, sb_ref)

    def run(first, width):
        for u in range(width):
            bufs[(u + 1) % 2][...] = scores(first + u + 1)
            absorb(bufs[u % 2], first + u, False)

    def quad(i, carry):
        run(ATTN_GROUP * i, ATTN_GROUP)
        return carry

    n_groups = n_full // ATTN_GROUP
    lax.fori_loop(0, n_groups, quad, 0)
    done = ATTN_GROUP * n_groups

    def pair(i, carry):
        run(done + 2 * i, 2)
        return carry

    n_pairs = (n_full - done) // 2
    lax.fori_loop(0, n_pairs, pair, 0)
    done = done + 2 * n_pairs
    left = n_need - done

    c1 = tk if (not diff and tq == 2 * tk) else 0

    @pl.when(left >= 1)
    def _():
        sb_ref[:, c1:] = scores(done + 1, c1)
        absorb(sa_ref, done, True)

    @pl.when(left >= 2)
    def _():
        absorb(sb_ref, done + 1, True, c1)

    def rest(j, carry):
        sa_ref[...] = scores(j)
        absorb(sa_ref, j, True)
        return carry

    lax.fori_loop(done + 2, n_need, rest, 0)

    o_t = acc_ref[...] / l_ref[...]
    if diff:
        lam = (jnp.exp(jnp.sum(lq1_ref[...] * lk1_ref[...], keepdims=True))
               - jnp.exp(jnp.sum(lq2_ref[...] * lk2_ref[...], keepdims=True)) + lam_init)
        od = o_t[:, 0:tq] - lam * o_t[:, tq:]
        od = od * lax.rsqrt(jnp.mean(od * od, axis=0, keepdims=True) + EPS) * gd_ref[...]
        o_ref[0] = (od * (1.0 - lam_init)).T.astype(o_ref.dtype)
    else:
        o_ref[0] = o_t.T.astype(o_ref.dtype)


def _attention_t(q, k, v_t, heads, tq, diff_args=None, lam_init=0.0):
    b, sq, _ = q.shape
    nk, tk = v_t.shape[2], v_t.shape[4]
    diff = diff_args is not None
    in_specs = [pl.BlockSpec((1, tq, LANES), lambda i, h, qi: (i, qi, h)),
                pl.BlockSpec((1, nk * tk, LANES), lambda i, h, qi: (i, 0, h)),
                pl.BlockSpec((1, 1, nk, LANES, tk), lambda i, h, qi: (i, h, 0, 0, 0))]
    cols = 2 * tq if diff else tq
    scratch = [pltpu.VMEM((1, cols), F32), pltpu.VMEM((1, cols), F32), pltpu.VMEM((LANES, cols), F32),
               pltpu.VMEM((tk, cols), F32), pltpu.VMEM((tk, cols), F32)]
    args = [q, k, v_t]
    if diff:
        lq1, lk1, lq2, lk2, g_diff = diff_args
        small = pl.BlockSpec((1, DIFF_HD), lambda i, h, qi: (0, 0))
        in_specs += [small] * 4 + [pl.BlockSpec((2 * DIFF_HD, 1), lambda i, h, qi: (0, 0))]
        scratch.append(pltpu.VMEM((cols, LANES), BF16))
        args += [lq1, lk1, lq2, lk2, g_diff.reshape(-1, 1)]
    kern = functools.partial(_attn_t_kernel, tq=tq, tk=tk, nk=nk, diff=diff, lam_init=lam_init)
    return pl.pallas_call(
        kern,
        grid=(b, heads, sq // tq),
        in_specs=in_specs,
        out_specs=pl.BlockSpec((1, tq, LANES), lambda i, h, qi: (i, qi, h)),
        out_shape=jax.ShapeDtypeStruct((b, sq, heads * LANES), BF16),
        scratch_shapes=scratch,
        compiler_params=_cparams(("arbitrary",) * 3),
    )(*args)


def _attn_kernel(*refs, tq, tk, nk, q_off, diff, lam_init):
    if diff:
        (q_ref, k_ref, v_ref, lq1_ref, lk1_ref, lq2_ref, lk2_ref, gd_ref,
         o_ref, m_ref, l_ref, acc_ref, qs_ref) = refs
    else:
        q_ref, k_ref, v_ref, o_ref, m_ref, l_ref, acc_ref = refs
    qi = pl.program_id(2)
    rows = 2 * tq if diff else tq

    m_ref[...] = jnp.full(m_ref.shape, NEG_BIG, F32)
    l_ref[...] = jnp.zeros(l_ref.shape, F32)
    acc_ref[...] = jnp.zeros(acc_ref.shape, F32)
    if diff:
        q = q_ref[0]
        lane = lax.broadcasted_iota(I32, q.shape, 1)
        qs_ref[0:tq, :] = jnp.where(lane < DIFF_HD, q, jnp.zeros_like(q))
        qs_ref[tq:, :] = jnp.where(lane >= DIFF_HD, q, jnp.zeros_like(q))

    q_start = q_off + qi * tq
    seen_by_all = ((q_start >> CHUNK_SHIFT) + 1) << CHUNK_SHIFT
    seen_by_any = (((q_start + tq - 1) >> CHUNK_SHIFT) + 1) << CHUNK_SHIFT
    n_full = jnp.minimum(seen_by_all // tk, nk)
    n_need = jnp.minimum((seen_by_any + tk - 1) // tk, nk)

    def block(j, masked):
        start = pl.multiple_of(j * tk, tk)
        q = qs_ref[...] if diff else q_ref[0]
        s = lax.dot_general(q, k_ref[0, pl.ds(start, tk), :], (((1,), (1,)), ((), ())),
                            preferred_element_type=F32)
        if masked:
            r = lax.broadcasted_iota(I32, (rows, 1), 0)
            if diff:
                r = jnp.where(r >= tq, r - tq, r)
            q_chunk = (q_start + r) >> CHUNK_SHIFT
            k_chunk = (start + lax.broadcasted_iota(I32, (1, tk), 1)) >> CHUNK_SHIFT
            s = jnp.where(k_chunk <= q_chunk, s, -jnp.inf)
        m_prev = m_ref[...]
        m_new = jnp.maximum(m_prev, jnp.max(s, axis=1, keepdims=True))
        alpha = jnp.exp2(m_prev - m_new)
        p = jnp.exp2(s - m_new)
        l_ref[...] = alpha * l_ref[...] + jnp.sum(p, axis=1, keepdims=True)
        acc_ref[...] = alpha * acc_ref[...] + jnp.dot(p.astype(BF16), v_ref[0, pl.ds(start, tk), :],
                                                      preferred_element_type=F32)
        m_ref[...] = m_new

    def full_body(j, c):
        block(j, False)
        return c

    def masked_body(j, c):
        block(j, True)
        return c

    lax.fori_loop(0, n_full, full_body, 0)
    lax.fori_loop(n_full, n_need, masked_body, 0)

    o = acc_ref[...] / l_ref[...]
    if diff:
        lam = (jnp.exp(jnp.sum(lq1_ref[...] * lk1_ref[...], keepdims=True))
               - jnp.exp(jnp.sum(lq2_ref[...] * lk2_ref[...], keepdims=True)) + lam_init)
        od = o[0:tq] - lam * o[tq:]
        o_ref[0] = (_rms(od, gd_ref[...]) * (1.0 - lam_init)).astype(o_ref.dtype)
    else:
        o_ref[0] = o.astype(o_ref.dtype)


def _attention(q, k, v, heads, tq, tk, q_off, diff_args=None, lam_init=0.0):
    b, sq, _ = q.shape
    sk = k.shape[1]
    nq, nk = sq // tq, sk // tk
    diff = diff_args is not None
    kv_spec = pl.BlockSpec((1, sk, LANES), lambda i, h, qi: (i, 0, h))
    in_specs = [pl.BlockSpec((1, tq, LANES), lambda i, h, qi: (i, qi, h)), kv_spec, kv_spec]
    rows = 2 * tq if diff else tq
    scratch = [pltpu.VMEM((rows, 1), F32), pltpu.VMEM((rows, 1), F32), pltpu.VMEM((rows, LANES), F32)]
    args = [q, k, v]
    if diff:
        small = lambda w: pl.BlockSpec((1, w), lambda i, h, qi: (0, 0))
        in_specs += [small(DIFF_HD)] * 4 + [small(2 * DIFF_HD)]
        scratch.append(pltpu.VMEM((rows, LANES), BF16))
        args += list(diff_args)
    kern = functools.partial(_attn_kernel, tq=tq, tk=tk, nk=nk, q_off=q_off, diff=diff, lam_init=lam_init)
    return pl.pallas_call(
        kern,
        grid=(b, heads, nq),
        in_specs=in_specs,
        out_specs=pl.BlockSpec((1, tq, LANES), lambda i, h, qi: (i, qi, h)),
        out_shape=jax.ShapeDtypeStruct((b, sq, heads * LANES), BF16),
        scratch_shapes=scratch,
        compiler_params=_cparams(("arbitrary",) * 3),
    )(*args)


def _out_kernel(x_ref, om_ref, od_ref, wout_ref, mod_ref, gf_ref, wr_ref, br_ref,
                x1_ref, hp_ref, idx_ref, gate_ref):
    a = (jnp.dot(om_ref[0], wout_ref[0:MLA_PAD, :], preferred_element_type=F32)
         + jnp.dot(od_ref[0], wout_ref[MLA_PAD:, :], preferred_element_type=F32))
    mod = mod_ref[0]
    x1 = x_ref[0] + mod[2:3] * a
    x1_ref[0] = x1
    h2 = _rms(x1, gf_ref[...]) * (1.0 + mod[4:5]) + mod[3:4]
    bits = lax.bitcast_convert_type(h2.astype(BF16).astype(F32), U32)
    half = h2.shape[1] // 2
    hp_ref[0] = (bits[:, :half] >> 16) | (bits[:, half:] & jnp.uint32(0xFFFF0000))

    logits = lax.dot_general(wr_ref[...], h2, (((1,), (1,)), ((), ())), precision=lax.Precision.HIGHEST,
                             preferred_element_type=F32) + br_ref[...]
    e_iota = lax.broadcasted_iota(I32, logits.shape, 0).astype(F32)
    vals, idxs = [], []
    for _ in range(TOP_K):
        m = jnp.max(logits, axis=0, keepdims=True)
        i = jnp.min(jnp.where(logits == m, e_iota, float(N_EXPERTS)), axis=0, keepdims=True)
        vals.append(m)
        idxs.append(i)
        logits = jnp.where(e_iota == i, -jnp.inf, logits)
    ex = [jnp.exp(v - vals[0]) for v in vals]
    tot = ex[0] + ex[1] + ex[2] + ex[3]
    idx_ref[0] = jnp.concatenate(idxs, axis=0).astype(I32)
    gate_ref[0] = jnp.concatenate([e / tot for e in ex], axis=0)


def _out(x, om, od, w_out_p, mod, g_ffn, w_r, b_r, t):
    b, s, d = x.shape
    ns = s // t
    full = lambda shape: pl.BlockSpec(shape, lambda i, j: (0,) * len(shape))
    tok = lambda w: pl.BlockSpec((1, t, w), lambda i, j: (i, j, 0))
    flat = pl.BlockSpec((1, TOP_K, t), lambda i, j: (i * ns + j, 0, 0))
    x1, hp, idx, gates = pl.pallas_call(
        _out_kernel,
        grid=(b, ns),
        in_specs=[tok(d), tok(MLA_PAD), tok(DIFF_QKV), full((MLA_PAD + DIFF_QKV, d)),
                  pl.BlockSpec((1, 6, d), lambda i, j: (i, 0, 0)), full((1, d)),
                  full((N_EXPERTS, d)), full((N_EXPERTS, 1))],
        out_specs=[tok(d), tok(d // 2), flat, flat],
        out_shape=[jax.ShapeDtypeStruct((b, s, d), F32), jax.ShapeDtypeStruct((b, s, d // 2), U32),
                   jax.ShapeDtypeStruct((b * ns, TOP_K, t), I32), jax.ShapeDtypeStruct((b * ns, TOP_K, t), F32)],
        compiler_params=_cparams(("arbitrary", "arbitrary")),
    )(x, om, od, w_out_p, mod, g_ffn, w_r, b_r)
    unflat = lambda a: a.transpose(1, 0, 2).reshape(TOP_K, b * s)
    return x1, hp, unflat(idx), unflat(gates)


def _rank_kernel(idx_ref, rank_ref, cnt_ref, carry_ref):
    step = pl.program_id(0)

    @pl.when(step == 0)
    def _():
        carry_ref[...] = jnp.zeros(carry_ref.shape, F32)

    idx = idx_ref[...]
    t = idx.shape[1]
    e_iota = lax.broadcasted_iota(I32, (N_EXPERTS, t), 0)
    hits = [e_iota == idx[k:k + 1, :] for k in range(TOP_K)]
    onehot = sum(h.astype(F32) for h in hits)
    earlier = (lax.broadcasted_iota(I32, (t, t), 0) < lax.broadcasted_iota(I32, (t, t), 1)).astype(BF16)
    before = jnp.dot(onehot.astype(BF16), earlier, preferred_element_type=F32) + carry_ref[...]
    rank_ref[...] = jnp.concatenate(
        [jnp.sum(jnp.where(h, before, 0.0), axis=0, keepdims=True) for h in hits], axis=0).astype(I32)
    carry_ref[...] = carry_ref[...] + jnp.sum(onehot, axis=1, keepdims=True)
    cnt_ref[...] = carry_ref[...].astype(I32)


def _rank(idx, t):
    n = idx.shape[1]
    return pl.pallas_call(
        _rank_kernel,
        grid=(n // t,),
        in_specs=[pl.BlockSpec((TOP_K, t), lambda i: (0, i))],
        out_specs=[pl.BlockSpec((TOP_K, t), lambda i: (0, i)), pl.BlockSpec((N_EXPERTS, 1), lambda i: (0, 0))],
        out_shape=[jax.ShapeDtypeStruct((TOP_K, n), I32), jax.ShapeDtypeStruct((N_EXPERTS, 1), I32)],
        scratch_shapes=[pltpu.VMEM((N_EXPERTS, 1), F32)],
        compiler_params=_cparams(("arbitrary",)),
    )(idx)


def _dispatch_kernel(dest_ref, dnxt_ref, hp_ref, xs_in_ref, xs_ref, dsm, sem, idx_sem, *, t, n_steps):
    del xs_in_ref
    i = pl.program_id(0)
    slot = i % 2

    @pl.when(i == 0)
    def _():
        cp = pltpu.make_async_copy(dest_ref.at[0], dsm.at[0], idx_sem)
        cp.start()
        cp.wait()

    nxt = pltpu.make_async_copy(dnxt_ref.at[0], dsm.at[1 - slot], idx_sem)

    @pl.when(i + 1 < n_steps)
    def _():
        nxt.start()

    def issue(tok, c):
        for k in range(TOP_K):
            pltpu.make_async_copy(hp_ref.at[pl.ds(tok, 1)], xs_ref.at[pl.ds(dsm[slot, k, tok], 1)], sem).start()
        return c

    lax.fori_loop(0, t, issue, 0, unroll=ISSUE_UNROLL)
    for k in range(TOP_K):
        pltpu.make_async_copy(hp_ref, xs_ref.at[pl.ds(0, t)], sem).wait()

    @pl.when(i + 1 < n_steps)
    def _():
        nxt.wait()


def _dispatch(dest3, hp, n_slots, t):
    n, w = hp.shape
    n_steps = n // t
    zeros = jnp.zeros((n_slots, w), U32)
    return pl.pallas_call(
        functools.partial(_dispatch_kernel, t=t, n_steps=n_steps),
        grid=(n_steps,),
        in_specs=[pl.BlockSpec((1, TOP_K, t), lambda i: (i, 0, 0)),
                  pl.BlockSpec((1, TOP_K, t), lambda i: (jnp.minimum(i + 1, n_steps - 1), 0, 0)),
                  pl.BlockSpec((t, w), lambda i: (i, 0)),
                  pl.BlockSpec(memory_space=pl.ANY)],
        out_specs=pl.BlockSpec(memory_space=pl.ANY),
        out_shape=jax.ShapeDtypeStruct((n_slots, w), U32),
        scratch_shapes=[pltpu.SMEM((2, TOP_K, t), I32), pltpu.SemaphoreType.DMA, pltpu.SemaphoreType.DMA],
        input_output_aliases={3: 0},
        compiler_params=_cparams(("arbitrary",)),
    )(dest3, dest3, hp, zeros)


def _expert_kernel(te_ref, nu_ref, xs_ref, wgu_ref, bgu_ref, wd_ref, bd_ref, ys_ref):
    del te_ref

    @pl.when(pl.program_id(0) < nu_ref[0])
    def _():
        p = xs_ref[...]
        x_lo = lax.bitcast_convert_type(p << 16, F32).astype(BF16)
        x_hi = lax.bitcast_convert_type(p & jnp.uint32(0xFFFF0000), F32).astype(BF16)
        half = p.shape[1]
        gu = (jnp.dot(x_lo, wgu_ref[0, 0:half, :], preferred_element_type=F32)
              + jnp.dot(x_hi, wgu_ref[0, half:, :], preferred_element_type=F32) + bgu_ref[0])
        f = gu.shape[1] // 2
        glu = jnp.minimum(gu[:, :f], SWIGLU_LIMIT)
        lin = jnp.clip(gu[:, f:], -SWIGLU_LIMIT, SWIGLU_LIMIT)
        act = glu * jax.nn.sigmoid(SWIGLU_ALPHA * glu) * (lin + 1.0)
        ys_ref[...] = jnp.dot(act.astype(BF16), wd_ref[0], preferred_element_type=F32) + bd_ref[0]


def _experts(tile_e, n_used, xs, w_gu, b_gu, w_down, b_down, tm):
    n_slots, half = xs.shape
    e, d, f2 = w_gu.shape
    n_tiles = n_slots // tm
    row = lambda i, te, nu: (jnp.minimum(i, nu[0] - 1), 0)
    wsel = lambda i, te, nu: (te[i], 0, 0)
    grid_spec = pltpu.PrefetchScalarGridSpec(
        num_scalar_prefetch=2,
        grid=(n_tiles,),
        in_specs=[pl.BlockSpec((tm, half), row),
                  pl.BlockSpec((1, d, f2), wsel), pl.BlockSpec((1, 1, f2), wsel),
                  pl.BlockSpec((1, f2 // 2, d), wsel), pl.BlockSpec((1, 1, d), wsel)],
        out_specs=pl.BlockSpec((tm, d), row),
    )
    return pl.pallas_call(
        _expert_kernel,
        grid_spec=grid_spec,
        out_shape=jax.ShapeDtypeStruct((n_slots, d), F32),
        compiler_params=_cparams(("arbitrary",)),
    )(tile_e, n_used, xs, w_gu, b_gu.reshape(e, 1, f2), w_down, b_down.reshape(e, 1, d))


def _combine_kernel(dcur_ref, dnxt_ref, dnn_ref, gate_ref, x1_ref, mod_ref, gfin_ref, ys_ref,
                    x2_ref, buf, dsm, sem, idx_sem, *, t, n_steps, final):
    i = pl.program_id(0)
    slot = i % 2

    def load_idx(src, s):
        cp = pltpu.make_async_copy(src.at[0], dsm.at[s], idx_sem)
        cp.start()
        cp.wait()

    def issue(s, si):
        def body(tok, c):
            for k in range(TOP_K):
                pltpu.make_async_copy(ys_ref.at[pl.ds(dsm[si, k, tok], 1)], buf.at[s, k, pl.ds(tok, 1)],
                                      sem.at[s]).start()
            return c
        lax.fori_loop(0, t, body, 0, unroll=ISSUE_UNROLL)

    @pl.when(i == 0)
    def _():
        load_idx(dcur_ref, 0)
        issue(0, 0)
        if n_steps > 1:
            load_idx(dnxt_ref, 1)

    ahead = pltpu.make_async_copy(dnn_ref.at[0], dsm.at[(i + 2) % 3], idx_sem)

    @pl.when(i + 2 < n_steps)
    def _():
        ahead.start()

    @pl.when(i + 1 < n_steps)
    def _():
        issue(1 - slot, (i + 1) % 3)

    for k in range(TOP_K):
        pltpu.make_async_copy(ys_ref.at[pl.ds(0, t)], buf.at[slot, k], sem.at[slot]).wait()

    g = gate_ref[...]
    y = g[:, 0:1] * buf[slot, 0]
    for k in range(1, TOP_K):
        y = y + g[:, k:k + 1] * buf[slot, k]
    x2 = x1_ref[...] + mod_ref[0][5:6] * y
    x2_ref[...] = _rms(x2, gfin_ref[...]) if final else x2

    @pl.when(i + 2 < n_steps)
    def _():
        ahead.wait()


def _combine(dest3, gates_t, x1, mod, g_final, ys, s, t, final):
    n, d = x1.shape
    n_steps = n // t
    per_b = s // t
    return pl.pallas_call(
        functools.partial(_combine_kernel, t=t, n_steps=n_steps, final=final),
        grid=(n_steps,),
        in_specs=[pl.BlockSpec((1, TOP_K, t), lambda i: (i, 0, 0)),
                  pl.BlockSpec((1, TOP_K, t), lambda i: (jnp.minimum(i + 1, n_steps - 1), 0, 0)),
                  pl.BlockSpec((1, TOP_K, t), lambda i: (jnp.minimum(i + 2, n_steps - 1), 0, 0)),
                  pl.BlockSpec((t, TOP_K), lambda i: (i, 0)),
                  pl.BlockSpec((t, d), lambda i: (i, 0)),
                  pl.BlockSpec((1, 6, d), lambda i: (i // per_b, 0, 0)),
                  pl.BlockSpec((1, d), lambda i: (0, 0)),
                  pl.BlockSpec(memory_space=pl.ANY)],
        out_specs=pl.BlockSpec((t, d), lambda i: (i, 0)),
        out_shape=jax.ShapeDtypeStruct((n, d), F32),
        scratch_shapes=[pltpu.VMEM((2, TOP_K, t, d), F32), pltpu.SMEM((3, TOP_K, t), I32),
                        pltpu.SemaphoreType.DMA((2,)), pltpu.SemaphoreType.DMA],
        compiler_params=_cparams(("arbitrary",)),
    )(dest3, dest3, dest3, gates_t, x1, mod, g_final, ys)


def _rope_tables(pos, d, lane_pos):
    inv = ROPE_THETA ** (-jnp.arange(0, d, 2, dtype=F32) / d)
    ang = pos.astype(F32)[:, None] * inv[None, :]
    cos, sin = jnp.cos(ang), jnp.sin(ang)
    cos_t = jnp.ones((pos.shape[0], LANES), F32)
    sin_t = jnp.zeros((pos.shape[0], LANES), F32)
    for o in lane_pos:
        cos_t = cos_t.at[:, o:o + d].set(jnp.concatenate([cos, cos], axis=1))
        sin_t = sin_t.at[:, o:o + d].set(jnp.concatenate([-sin, sin], axis=1))
    return cos_t, sin_t


def _prep_weights(w_in, w_uq, w_uk, w_uv, w_out):
    depth, d, _ = w_in.shape
    cuts = [MLA_Q_LORA, MLA_KV_LORA, MLA_ROPE, DIFF_QKV, DIFF_QKV, DIFF_QKV]
    offs = [0]
    for c in cuts:
        offs.append(offs[-1] + c)
    parts = [w_in[:, :, offs[i]:offs[i + 1]] for i in range(6)]
    kr_pad = jnp.pad(parts[2], ((0, 0), (0, 0), (MLA_NOPE, LANES - MLA_NOPE - MLA_ROPE)))
    w_in_p = jnp.concatenate([parts[0], parts[1], kr_pad, parts[3], parts[4], parts[5]], axis=2).astype(BF16)
    qd = MLA_NOPE + MLA_ROPE
    w_uq_p = jnp.pad(w_uq.reshape(depth, MLA_Q_LORA, MLA_HEADS, qd), ((0, 0), (0, 0), (0, 0), (0, LANES - qd)))
    w_uq_p = w_uq_p.reshape(depth, MLA_Q_LORA, MLA_PAD).astype(BF16)
    pad_head = lambda w, n: jnp.pad(w, ((0, 0), (0, 0), (0, 0), (0, LANES - n))).reshape(depth, MLA_KV_LORA, MLA_PAD)
    w_ukv_p = jnp.concatenate([pad_head(w_uk, MLA_NOPE), pad_head(w_uv, MLA_V)], axis=2).astype(BF16)
    wo_m = jnp.pad(w_out[:, :MLA_HEADS * MLA_V].reshape(depth, MLA_HEADS, MLA_V, d),
                   ((0, 0), (0, 0), (0, LANES - MLA_V), (0, 0))).reshape(depth, MLA_PAD, d)
    w_out_p = jnp.concatenate([wo_m, w_out[:, MLA_HEADS * MLA_V:]], axis=1).astype(BF16)
    place = jnp.zeros((MLA_ROPE, 2 * MLA_PAD), F32)
    eye = jnp.eye(MLA_ROPE, dtype=F32)
    for hd in range(MLA_HEADS):
        place = place.at[:, hd * LANES + MLA_NOPE:hd * LANES + MLA_NOPE + MLA_ROPE].set(eye)
    w_uvt_p = jnp.swapaxes(w_ukv_p[:, :, MLA_PAD:], 1, 2)
    return w_in_p, w_uq_p, w_ukv_p, w_out_p, place.astype(BF16), w_uvt_p


def _pick(n, pref):
    t = min(n, pref)
    while n % t:
        t //= 2
    return t


def _moe(hp, idx, gates, x1, mod_l, g_final, wts, s, final):
    w_gu, b_gu, w_down, b_down = wts
    n = hp.shape[0]
    rank, counts = _rank(idx, _pick(n, RANK_TILE))
    counts = counts[:, 0]
    tm = MOE_TILE if n * TOP_K >= N_EXPERTS * MOE_TILE else MOE_TILE_SMALL
    padded = (counts + tm - 1) // tm * tm
    pad_ends = jnp.cumsum(padded)
    pad_starts = pad_ends - padded
    dest = rank + jnp.sum(jnp.where(idx[None] == jnp.arange(N_EXPERTS, dtype=I32)[:, None, None],
                                    pad_starts[:, None, None], 0), axis=0)
    n_tiles = -(-(n * TOP_K) // tm) + N_EXPERTS
    tile_e = jnp.minimum(jnp.sum(pad_ends[None, :] <= (jnp.arange(n_tiles, dtype=I32) * tm)[:, None], axis=1),
                         N_EXPERTS - 1).astype(I32)
    n_used = (pad_ends[-1:] // tm).astype(I32)
    td = _pick(n, DISPATCH_TILE)
    xs = _dispatch(dest.reshape(TOP_K, n // td, td).transpose(1, 0, 2), hp, n_tiles * tm, td)
    ys = _experts(tile_e, n_used, xs, w_gu, b_gu, w_down, b_down, tm)
    tc = _pick(s, COMBINE_TILE)
    return _combine(dest.reshape(TOP_K, n // tc, tc).transpose(1, 0, 2), gates.T, x1, mod_l, g_final, ys, s, tc,
                    final)


def _trunk(x, c, caches, prm, prepped, q_off):
    b, s, d = x.shape
    depth = prm['w_in'].shape[0]
    w_in_p, w_uq_p, w_ukv_p, w_out_p, place, w_uvt_p, w_ada_b, w_gu_b, w_down_b = prepped
    rows = -(-b // 16) * 16
    mod = _ada(jnp.pad(c, ((0, rows - b), (0, 0))), w_ada_b, prm['b_ada'])[:, :b].reshape(depth, b, 6, d)
    pos = q_off + jnp.arange(s, dtype=I32)
    cos_q, sin_q = _rope_tables(pos, MLA_ROPE, (MLA_NOPE,))
    cos_d, sin_d = _rope_tables(pos, DIFF_HD, (0, DIFF_HD))
    t = _pick(s, TOK_TILE)
    stacks = tuple(jnp.zeros((depth, b, s, w), F32) for w in ROW_WIDTHS)
    row = lambda v: v.reshape(1, -1)
    for l in range(depth):
        qm, qd, stacks = _proj(x, mod[l], row(prm['g_attn'][l]), w_in_p[l], row(prm['g_q'][l]), w_uq_p[l],
                               row(prm['g_kv'][l]), (cos_q, sin_q, cos_d, sin_d), t, l, depth, stacks)
        lam_init = 0.8 - 0.6 * math.exp(-0.3 * l)
        diff_args = (row(prm['lam_q1'][l]), row(prm['lam_k1'][l]), row(prm['lam_q2'][l]),
                     row(prm['lam_k2'][l]), row(prm['g_diff'][l]))
        if caches is None and q_off == 0 and s % ATTN_TK == 0:
            km, vm_t, dkb, dv_t = _kvt(stacks, l, w_ukv_p[l], place, w_uvt_p[l], ATTN_TK)
            om = _attention_t(qm, km, vm_t, MLA_HEADS, _pick(s, ATTN_TQ))
            od = _attention_t(qd, dkb, dv_t, DIFF_HEADS, _pick(s, ATTN_TQ // 2), diff_args, lam_init)
        else:
            if caches is not None:
                lat_a, kr_a, dk_a, dv_a = (jnp.concatenate([cc[l].reshape(b, cc.shape[2], -1), st[l]], axis=1)
                                           for cc, st in zip(caches, stacks))
            else:
                lat_a, kr_a, dk_a, dv_a = (st[l] for st in stacks)
            sk = lat_a.shape[1]
            tk = ATTN_TK if sk % ATTN_TK == 0 else sk
            tkv = max(t_ for t_ in range(16, min(sk, 2 * TOK_TILE) + 1, 16) if sk % t_ == 0)
            km, vm, dkb, dvb = _kv(lat_a, kr_a, dk_a, dv_a, w_ukv_p[l], place, tkv)
            om = _attention(qm, km, vm, MLA_HEADS, _pick(s, ATTN_TQ), tk, q_off)
            od = _attention(qd, dkb, dvb, DIFF_HEADS, _pick(s, ATTN_TQ // 2), tk, q_off, diff_args, lam_init)
        x1, hp, idx, gates = _out(x, om, od, w_out_p[l], mod[l], row(prm['g_ffn'][l]),
                                  prm['w_router'][l].T, prm['b_router'][l].reshape(-1, 1), t)
        x = _moe(hp.reshape(b * s, d // 2), idx, gates, x1.reshape(b * s, d), mod[l], row(prm['g_final']),
                 (w_gu_b[l], prm['b_gate_up'][l], w_down_b[l], prm['b_down'][l]), s, l == depth - 1).reshape(b, s, d)
    lat_n, kr_n, dk_n, dv_n = stacks
    shp = (depth, b, s, DIFF_HEADS, 2 * DIFF_HD)
    return x, (lat_n, kr_n, dk_n.reshape(shp), dv_n.reshape(shp))


def kernel(x_prompt, x_sample, c_prompt, c_sample, cache_mla_latent, cache_mla_krope, cache_diff_k, cache_diff_v,
           w_ada, b_ada, g_attn, g_ffn, w_in, g_q, w_uq, g_kv, w_uk, w_uv, lam_q1, lam_k1, lam_q2, lam_k2, g_diff,
           w_out, w_router, b_router, w_gate_up, b_gate_up, w_down, b_down, g_final):
    prm = {'b_ada': b_ada, 'g_attn': g_attn, 'g_ffn': g_ffn, 'w_in': w_in, 'g_q': g_q, 'g_kv': g_kv,
           'lam_q1': lam_q1, 'lam_k1': lam_k1, 'lam_q2': lam_q2, 'lam_k2': lam_k2, 'g_diff': g_diff,
           'w_router': w_router, 'b_router': b_router, 'b_gate_up': b_gate_up, 'b_down': b_down,
           'g_final': g_final}
    prepped = _prep_weights(w_in, w_uq, w_uk, w_uv, w_out) + (
        w_ada.astype(BF16), w_gate_up.astype(BF16), w_down.astype(BF16))
    y_p, rows_p = _trunk(x_prompt, c_prompt, None, prm, prepped, 0)
    caches = (cache_mla_latent, cache_mla_krope, cache_diff_k, cache_diff_v)
    y_s, rows_s = _trunk(x_sample, c_sample, caches, prm, prepped, cache_mla_latent.shape[2])
    return (y_p, y_s) + rows_p + rows_s
```

```python
import functools
import math

import jax
import jax.numpy as jnp
from jax import lax
from jax.experimental import pallas as pl
from jax.experimental.pallas import tpu as pltpu

F32 = jnp.float32
BF16 = jnp.bfloat16
I32 = jnp.int32
U32 = jnp.uint32

CHUNK = 64
ROPE_THETA = 10000.0
EPS = 1e-6
MLA_HEADS = 8
MLA_Q_LORA = 256
MLA_KV_LORA = 128
MLA_NOPE = 64
MLA_ROPE = 32
MLA_V = 64
DIFF_HEADS = 4
DIFF_HD = 64
DIFF_QKV = DIFF_HEADS * 2 * DIFF_HD
N_EXPERTS = 32
TOP_K = 4
SWIGLU_LIMIT = 7.0
SWIGLU_ALPHA = 1.702

LANES = 128
MLA_PAD = MLA_HEADS * LANES
PROJ_COLS = MLA_Q_LORA + MLA_KV_LORA + LANES + 3 * DIFF_QKV
VMEM_LIMIT = 56 * 1024 * 1024
MOE_TILE = 512
MOE_TILE_SMALL = 128
TOK_TILE = 512
ATTN_TQ = 1024
ATTN_TK = 512
ATTN_GROUP = 4
RANK_TILE = 512
DISPATCH_TILE = 512
COMBINE_TILE = 256
ISSUE_UNROLL = 8
NEG_BIG = -1e30
LOG2E = math.log2(math.e)
assert CHUNK & (CHUNK - 1) == 0
CHUNK_SHIFT = CHUNK.bit_length() - 1


def _cparams(sem):
    return pltpu.CompilerParams(dimension_semantics=sem, vmem_limit_bytes=VMEM_LIMIT)


def _rms(x, g):
    return x * lax.rsqrt(jnp.mean(x * x, axis=-1, keepdims=True) + EPS) * g


def _rope_lanes(v, cos, sin_signed, half):
    lane = lax.broadcasted_iota(I32, v.shape, 1)
    first = (lane % (2 * half)) < half
    swapped = jnp.where(first, pltpu.roll(v, LANES - half, 1), pltpu.roll(v, half, 1))
    return v * cos + swapped * sin_signed


def _ada_kernel(c_ref, w_ref, b_ref, o_ref):
    c = c_ref[...]
    s = c * jax.nn.sigmoid(c)
    o_ref[0] = jnp.dot(s.astype(BF16), w_ref[0], preferred_element_type=F32) + b_ref[0]


def _ada(c, w_ada, b_ada):
    depth, d, n6 = w_ada.shape
    rows = c.shape[0]
    tn = 1536
    return pl.pallas_call(
        _ada_kernel,
        grid=(depth, n6 // tn),
        in_specs=[pl.BlockSpec((rows, d), lambda l, j: (0, 0)),
                  pl.BlockSpec((1, d, tn), lambda l, j: (l, 0, j)),
                  pl.BlockSpec((1, 1, tn), lambda l, j: (l, 0, j))],
        out_specs=pl.BlockSpec((1, rows, tn), lambda l, j: (l, 0, j)),
        out_shape=jax.ShapeDtypeStruct((depth, rows, n6), F32),
        compiler_params=_cparams(("arbitrary", "arbitrary")),
    )(c, w_ada, b_ada.reshape(depth, 1, n6))


def _proj_kernel(x_ref, mod_ref, ga_ref, win_ref, gq_ref, wuq_ref, gkv_ref,
                 cq_ref, sq_ref, cd_ref, sd_ref, *rest, with_kv):
    outs = rest[-10:] if with_kv else rest[-6:]
    qm_ref, qd_ref, lat_ref, kr_ref, dk_ref, dv_ref = outs[:6]
    x = x_ref[0]
    mod = mod_ref[0]
    h = _rms(x, ga_ref[...]) * (1.0 + mod[1:2]) + mod[0:1]
    proj = jnp.dot(h.astype(BF16), win_ref[...], preferred_element_type=F32)
    o = 0
    cq = proj[:, o:o + MLA_Q_LORA]; o += MLA_Q_LORA
    ckv = proj[:, o:o + MLA_KV_LORA]; o += MLA_KV_LORA
    krp = proj[:, o:o + LANES]; o += LANES
    dq = proj[:, o:o + DIFF_QKV]; o += DIFF_QKV
    dk = proj[:, o:o + DIFF_QKV]; o += DIFF_QKV
    dv = proj[:, o:o + DIFF_QKV]

    cos_q, sin_q = cq_ref[...], sq_ref[...]
    cos_d, sin_d = cd_ref[...], sd_ref[...]
    q = jnp.dot(_rms(cq, gq_ref[...]).astype(BF16), wuq_ref[...], preferred_element_type=F32)
    q_scale = LOG2E / math.sqrt(MLA_NOPE + MLA_ROPE)
    for hd in range(MLA_HEADS):
        sl = slice(hd * LANES, (hd + 1) * LANES)
        qm_ref[0, :, sl] = (_rope_lanes(q[:, sl], cos_q, sin_q, MLA_ROPE // 2) * q_scale).astype(BF16)
    lat = _rms(ckv, gkv_ref[...])
    lat_ref[0, 0] = lat
    kr_lanes = _rope_lanes(krp, cos_q, sin_q, MLA_ROPE // 2)
    kr_ref[0, 0] = kr_lanes[:, MLA_NOPE:MLA_NOPE + MLA_ROPE]
    d_scale = LOG2E / math.sqrt(DIFF_HD)
    for hd in range(DIFF_HEADS):
        sl = slice(hd * LANES, (hd + 1) * LANES)
        qd_ref[0, :, sl] = (_rope_lanes(dq[:, sl], cos_d, sin_d, DIFF_HD // 2) * d_scale).astype(BF16)
        dk_h = _rope_lanes(dk[:, sl], cos_d, sin_d, DIFF_HD // 2)
        dk_ref[0, 0, :, sl] = dk_h
        if with_kv:
            outs[8][0, :, sl] = dk_h.astype(BF16)
    dv_ref[0, 0] = dv
    if with_kv:
        wuk_ref, wuvt_ref, eye_ref = rest[:3]
        km_ref, vt_ref, _, dvt_ref = outs[6:]
        lat_b = lat.astype(BF16)
        k_nope = jnp.dot(lat_b, wuk_ref[...], preferred_element_type=F32)
        for hd in range(MLA_HEADS):
            sl = slice(hd * LANES, (hd + 1) * LANES)
            km_ref[0, :, sl] = (k_nope[:, sl] + kr_lanes).astype(BF16)
        nt = (((1,), (1,)), ((), ()))
        v_t = lax.dot_general(wuvt_ref[...], lat_b, nt, preferred_element_type=F32)
        for hd in range(MLA_HEADS):
            vt_ref[0, hd, 0] = v_t[hd * LANES:(hd + 1) * LANES, :].astype(BF16)
        dv_t = lax.dot_general(eye_ref[...], dv.astype(BF16), nt, preferred_element_type=F32)
        for hd in range(DIFF_HEADS):
            dvt_ref[0, hd, 0] = dv_t[hd * LANES:(hd + 1) * LANES, :].astype(BF16)


ROW_WIDTHS = (MLA_KV_LORA, MLA_ROPE, DIFF_QKV, DIFF_QKV)


def _proj(x, mod, g_attn, w_in_p, g_q, w_uq_p, g_kv, tabs, t, layer, depth, stacks, kv_weights=None):
    b, s, d = x.shape
    cos_q, sin_q, cos_d, sin_d = tabs
    full = lambda shape: pl.BlockSpec(shape, lambda i, j: (0,) * len(shape))
    tok = lambda w: pl.BlockSpec((1, t, w), lambda i, j: (i, j, 0))
    tab = pl.BlockSpec((t, LANES), lambda i, j: (j, 0))
    lay = lambda w: pl.BlockSpec((1, 1, t, w), lambda i, j: (layer, i, j, 0))
    tr = lambda h: pl.BlockSpec((1, h, 1, LANES, t), lambda i, j: (i, 0, j, 0, 0))
    in_specs = [tok(d), pl.BlockSpec((1, 6, d), lambda i, j: (i, 0, 0)), full((1, d)),
                full((d, PROJ_COLS)), full((1, MLA_Q_LORA)), full((MLA_Q_LORA, MLA_PAD)),
                full((1, MLA_KV_LORA)), tab, tab, tab, tab]
    args = [x, mod, g_attn, w_in_p, g_q, w_uq_p, g_kv, cos_q, sin_q, cos_d, sin_d]
    out_specs = [tok(MLA_PAD), tok(DIFF_QKV)] + [lay(w) for w in ROW_WIDTHS]
    out_shape = ([jax.ShapeDtypeStruct((b, s, MLA_PAD), BF16), jax.ShapeDtypeStruct((b, s, DIFF_QKV), BF16)]
                 + [jax.ShapeDtypeStruct((depth, b, s, w), F32) for w in ROW_WIDTHS])
    with_kv = kv_weights is not None
    if with_kv:
        nk = s // t
        in_specs += [full((MLA_KV_LORA, MLA_PAD)), full((MLA_PAD, MLA_KV_LORA)), full((DIFF_QKV, DIFF_QKV))]
        args += [kv_weights[0], kv_weights[1], jnp.eye(DIFF_QKV, dtype=BF16)]
        out_specs += [tok(MLA_PAD), tr(MLA_HEADS), tok(DIFF_QKV), tr(DIFF_HEADS)]
        out_shape += [jax.ShapeDtypeStruct((b, s, MLA_PAD), BF16),
                      jax.ShapeDtypeStruct((b, MLA_HEADS, nk, LANES, t), BF16),
                      jax.ShapeDtypeStruct((b, s, DIFF_QKV), BF16),
                      jax.ShapeDtypeStruct((b, DIFF_HEADS, nk, LANES, t), BF16)]
    aliases = {len(args) + k: 2 + k for k in range(len(ROW_WIDTHS))}
    in_specs += [pl.BlockSpec(memory_space=pl.ANY)] * len(ROW_WIDTHS)
    args += list(stacks)
    out = pl.pallas_call(
        functools.partial(_proj_kernel, with_kv=with_kv),
        grid=(b, s // t),
        in_specs=in_specs,
        out_specs=out_specs,
        out_shape=out_shape,
        input_output_aliases=aliases,
        compiler_params=_cparams(("arbitrary", "arbitrary")),
    )(*args)
    return out[0], out[1], tuple(out[2:6]), tuple(out[6:])


def _kv_kernel(lat_ref, kr_ref, dk_ref, dv_ref, wukv_ref, place_ref, km_ref, vm_ref, dkb_ref, dvb_ref):
    kv = jnp.dot(lat_ref[0].astype(BF16), wukv_ref[...], preferred_element_type=F32)
    kv = kv + jnp.dot(kr_ref[0].astype(BF16), place_ref[...], preferred_element_type=F32)
    km_ref[0] = kv[:, :MLA_PAD].astype(BF16)
    vm_ref[0] = kv[:, MLA_PAD:].astype(BF16)
    dkb_ref[0] = dk_ref[0].astype(BF16)
    dvb_ref[0] = dv_ref[0].astype(BF16)


def _kv(lat, kr, dk, dv, w_ukv_p, place, t):
    b, s, _ = lat.shape
    full = lambda shape: pl.BlockSpec(shape, lambda i, j: (0,) * len(shape))
    tok = lambda w: pl.BlockSpec((1, t, w), lambda i, j: (i, j, 0))
    return pl.pallas_call(
        _kv_kernel,
        grid=(b, s // t),
        in_specs=[tok(MLA_KV_LORA), tok(MLA_ROPE), tok(DIFF_QKV), tok(DIFF_QKV),
                  full((MLA_KV_LORA, 2 * MLA_PAD)), full((MLA_ROPE, 2 * MLA_PAD))],
        out_specs=[tok(MLA_PAD), tok(MLA_PAD), tok(DIFF_QKV), tok(DIFF_QKV)],
        out_shape=[jax.ShapeDtypeStruct((b, s, MLA_PAD), BF16), jax.ShapeDtypeStruct((b, s, MLA_PAD), BF16),
                   jax.ShapeDtypeStruct((b, s, DIFF_QKV), BF16), jax.ShapeDtypeStruct((b, s, DIFF_QKV), BF16)],
        compiler_params=_cparams(("arbitrary", "arbitrary")),
    )(lat, kr, dk, dv, w_ukv_p, place)


def _attn_t_kernel(*refs, tq, tk, nk, diff, lam_init):
    if diff:
        (q_ref, k_ref, vt_ref, lq1_ref, lk1_ref, lq2_ref, lk2_ref, gd_ref,
         o_ref, m_ref, l_ref, acc_ref, sa_ref, sb_ref, qs_ref) = refs
    else:
        q_ref, k_ref, vt_ref, o_ref, m_ref, l_ref, acc_ref, sa_ref, sb_ref = refs
    qi = pl.program_id(2)
    cols = 2 * tq if diff else tq

    m_ref[...] = jnp.full(m_ref.shape, NEG_BIG, F32)
    l_ref[...] = jnp.zeros(l_ref.shape, F32)
    acc_ref[...] = jnp.zeros(acc_ref.shape, F32)
    if diff:
        q = q_ref[0]
        lane = lax.broadcasted_iota(I32, q.shape, 1)
        qs_ref[0:tq, :] = jnp.where(lane < DIFF_HD, q, jnp.zeros_like(q))
        qs_ref[tq:, :] = jnp.where(lane >= DIFF_HD, q, jnp.zeros_like(q))

    q_start = qi * tq
    seen_by_all = ((q_start >> CHUNK_SHIFT) + 1) << CHUNK_SHIFT
    seen_by_any = (((q_start + tq - 1) >> CHUNK_SHIFT) + 1) << CHUNK_SHIFT
    n_full = jnp.minimum(seen_by_all // tk, nk)
    n_need = jnp.minimum((seen_by_any + tk - 1) // tk, nk)

    def scores(j, c0=0):
        start = pl.multiple_of(jnp.minimum(j, nk - 1) * tk, tk)
        q = qs_ref[c0:, :] if diff else q_ref[0, c0:, :]
        return lax.dot_general(k_ref[0, pl.ds(start, tk), :], q, (((1,), (1,)), ((), ())),
                               preferred_element_type=F32)

    def absorb(s_ref, j, masked, c0=0):
        s = s_ref[:, c0:]
        if masked:
            c = c0 + lax.broadcasted_iota(I32, (1, cols - c0), 1)
            if diff:
                c = jnp.where(c >= tq, c - tq, c)
            q_chunk = (q_start + c) >> CHUNK_SHIFT
            k_chunk = (j * tk + lax.broadcasted_iota(I32, (tk, 1), 0)) >> CHUNK_SHIFT
            s = jnp.where(k_chunk <= q_chunk, s, -jnp.inf)
        m_prev = m_ref[:, c0:]
        m_new = jnp.maximum(m_prev, jnp.max(s, axis=0, keepdims=True))
        alpha = jnp.exp2(m_prev - m_new)
        p = jnp.exp2(s - m_new)
        l_ref[:, c0:] = alpha * l_ref[:, c0:] + jnp.sum(p, axis=0, keepdims=True)
        acc_ref[:, c0:] = alpha * acc_ref[:, c0:] + jnp.dot(vt_ref[0, 0, j], p.astype(BF16),
                                                            preferred_element_type=F32)
        m_ref[:, c0:] = m_new

    sa_ref[...] = scores(0)
    bufs = (sa_ref, sb_ref)

    def run(first, width):
        for u in range(width):
            bufs[(u + 1) % 2][...] = scores(first + u + 1)
            absorb(bufs[u % 2], first + u, False)

    def quad(i, carry):
        run(ATTN_GROUP * i, ATTN_GROUP)
        return carry

    n_groups = n_full // ATTN_GROUP
    lax.fori_loop(0, n_groups, quad, 0)
    done = ATTN_GROUP * n_groups

    def pair(i, carry):
        run(done + 2 * i, 2)
        return carry

    n_pairs = (n_full - done) // 2
    lax.fori_loop(0, n_pairs, pair, 0)
    done = done + 2 * n_pairs
    left = n_need - done

    c1 = tk if (not diff and tq == 2 * tk) else 0

    @pl.when(left >= 1)
    def _():
        sb_ref[:, c1:] = scores(done + 1, c1)
        absorb(sa_ref, done, True)

    @pl.when(left >= 2)
    def _():
        absorb(sb_ref, done + 1, True, c1)

    def rest(j, carry):
        sa_ref[...] = scores(j)
        absorb(sa_ref, j, True)
        return carry

    lax.fori_loop(done + 2, n_need, rest, 0)

    o_t = acc_ref[...] / l_ref[...]
    if diff:
        lam = (jnp.exp(jnp.sum(lq1_ref[...] * lk1_ref[...], keepdims=True))
               - jnp.exp(jnp.sum(lq2_ref[...] * lk2_ref[...], keepdims=True)) + lam_init)
        od = o_t[:, 0:tq] - lam * o_t[:, tq:]
        od = od * lax.rsqrt(jnp.mean(od * od, axis=0, keepdims=True) + EPS) * gd_ref[...]
        o_ref[0] = (od * (1.0 - lam_init)).T.astype(o_ref.dtype)
    else:
        o_ref[0] = o_t.T.astype(o_ref.dtype)


def _attention_t(q, k, v_t, heads, tq, diff_args=None, lam_init=0.0):
    b, sq, _ = q.shape
    nk, tk = v_t.shape[2], v_t.shape[4]
    diff = diff_args is not None
    in_specs = [pl.BlockSpec((1, tq, LANES), lambda i, h, qi: (i, qi, h)),
                pl.BlockSpec((1, nk * tk, LANES), lambda i, h, qi: (i, 0, h)),
                pl.BlockSpec((1, 1, nk, LANES, tk), lambda i, h, qi: (i, h, 0, 0, 0))]
    cols = 2 * tq if diff else tq
    scratch = [pltpu.VMEM((1, cols), F32), pltpu.VMEM((1, cols), F32), pltpu.VMEM((LANES, cols), F32),
               pltpu.VMEM((tk, cols), F32), pltpu.VMEM((tk, cols), F32)]
    args = [q, k, v_t]
    if diff:
        lq1, lk1, lq2, lk2, g_diff = diff_args
        small = pl.BlockSpec((1, DIFF_HD), lambda i, h, qi: (0, 0))
        in_specs += [small] * 4 + [pl.BlockSpec((2 * DIFF_HD, 1), lambda i, h, qi: (0, 0))]
        scratch.append(pltpu.VMEM((cols, LANES), BF16))
        args += [lq1, lk1, lq2, lk2, g_diff.reshape(-1, 1)]
    kern = functools.partial(_attn_t_kernel, tq=tq, tk=tk, nk=nk, diff=diff, lam_init=lam_init)
    return pl.pallas_call(
        kern,
        grid=(b, heads, sq // tq),
        in_specs=in_specs,
        out_specs=pl.BlockSpec((1, tq, LANES), lambda i, h, qi: (i, qi, h)),
        out_shape=jax.ShapeDtypeStruct((b, sq, heads * LANES), BF16),
        scratch_shapes=scratch,
        compiler_params=_cparams(("arbitrary",) * 3),
    )(*args)


def _attn_kernel(*refs, tq, tk, nk, q_off, diff, lam_init):
    if diff:
        (q_ref, k_ref, v_ref, lq1_ref, lk1_ref, lq2_ref, lk2_ref, gd_ref,
         o_ref, m_ref, l_ref, acc_ref, qs_ref) = refs
    else:
        q_ref, k_ref, v_ref, o_ref, m_ref, l_ref, acc_ref = refs
    qi = pl.program_id(2)
    rows = 2 * tq if diff else tq

    m_ref[...] = jnp.full(m_ref.shape, NEG_BIG, F32)
    l_ref[...] = jnp.zeros(l_ref.shape, F32)
    acc_ref[...] = jnp.zeros(acc_ref.shape, F32)
    if diff:
        q = q_ref[0]
        lane = lax.broadcasted_iota(I32, q.shape, 1)
        qs_ref[0:tq, :] = jnp.where(lane < DIFF_HD, q, jnp.zeros_like(q))
        qs_ref[tq:, :] = jnp.where(lane >= DIFF_HD, q, jnp.zeros_like(q))

    q_start = q_off + qi * tq
    seen_by_all = ((q_start >> CHUNK_SHIFT) + 1) << CHUNK_SHIFT
    seen_by_any = (((q_start + tq - 1) >> CHUNK_SHIFT) + 1) << CHUNK_SHIFT
    n_full = jnp.minimum(seen_by_all // tk, nk)
    n_need = jnp.minimum((seen_by_any + tk - 1) // tk, nk)

    def block(j, masked):
        start = pl.multiple_of(j * tk, tk)
        q = qs_ref[...] if diff else q_ref[0]
        s = lax.dot_general(q, k_ref[0, pl.ds(start, tk), :], (((1,), (1,)), ((), ())),
                            preferred_element_type=F32)
        if masked:
            r = lax.broadcasted_iota(I32, (rows, 1), 0)
            if diff:
                r = jnp.where(r >= tq, r - tq, r)
            q_chunk = (q_start + r) >> CHUNK_SHIFT
            k_chunk = (start + lax.broadcasted_iota(I32, (1, tk), 1)) >> CHUNK_SHIFT
            s = jnp.where(k_chunk <= q_chunk, s, -jnp.inf)
        m_prev = m_ref[...]
        m_new = jnp.maximum(m_prev, jnp.max(s, axis=1, keepdims=True))
        alpha = jnp.exp2(m_prev - m_new)
        p = jnp.exp2(s - m_new)
        l_ref[...] = alpha * l_ref[...] + jnp.sum(p, axis=1, keepdims=True)
        acc_ref[...] = alpha * acc_ref[...] + jnp.dot(p.astype(BF16), v_ref[0, pl.ds(start, tk), :],
                                                      preferred_element_type=F32)
        m_ref[...] = m_new

    def full_body(j, c):
        block(j, False)
        return c

    def masked_body(j, c):
        block(j, True)
        return c

    lax.fori_loop(0, n_full, full_body, 0)
    lax.fori_loop(n_full, n_need, masked_body, 0)

    o = acc_ref[...] / l_ref[...]
    if diff:
        lam = (jnp.exp(jnp.sum(lq1_ref[...] * lk1_ref[...], keepdims=True))
               - jnp.exp(jnp.sum(lq2_ref[...] * lk2_ref[...], keepdims=True)) + lam_init)
        od = o[0:tq] - lam * o[tq:]
        o_ref[0] = (_rms(od, gd_ref[...]) * (1.0 - lam_init)).astype(o_ref.dtype)
    else:
        o_ref[0] = o.astype(o_ref.dtype)


def _attention(q, k, v, heads, tq, tk, q_off, diff_args=None, lam_init=0.0):
    b, sq, _ = q.shape
    sk = k.shape[1]
    nq, nk = sq // tq, sk // tk
    diff = diff_args is not None
    kv_spec = pl.BlockSpec((1, sk, LANES), lambda i, h, qi: (i, 0, h))
    in_specs = [pl.BlockSpec((1, tq, LANES), lambda i, h, qi: (i, qi, h)), kv_spec, kv_spec]
    rows = 2 * tq if diff else tq
    scratch = [pltpu.VMEM((rows, 1), F32), pltpu.VMEM((rows, 1), F32), pltpu.VMEM((rows, LANES), F32)]
    args = [q, k, v]
    if diff:
        small = lambda w: pl.BlockSpec((1, w), lambda i, h, qi: (0, 0))
        in_specs += [small(DIFF_HD)] * 4 + [small(2 * DIFF_HD)]
        scratch.append(pltpu.VMEM((rows, LANES), BF16))
        args += list(diff_args)
    kern = functools.partial(_attn_kernel, tq=tq, tk=tk, nk=nk, q_off=q_off, diff=diff, lam_init=lam_init)
    return pl.pallas_call(
        kern,
        grid=(b, heads, nq),
        in_specs=in_specs,
        out_specs=pl.BlockSpec((1, tq, LANES), lambda i, h, qi: (i, qi, h)),
        out_shape=jax.ShapeDtypeStruct((b, sq, heads * LANES), BF16),
        scratch_shapes=scratch,
        compiler_params=_cparams(("arbitrary",) * 3),
    )(*args)


def _out_kernel(x_ref, om_ref, od_ref, wout_ref, mod_ref, gf_ref, wr_ref, br_ref,
                x1_ref, hp_ref, idx_ref, gate_ref):
    a = (jnp.dot(om_ref[0], wout_ref[0:MLA_PAD, :], preferred_element_type=F32)
         + jnp.dot(od_ref[0], wout_ref[MLA_PAD:, :], preferred_element_type=F32))
    mod = mod_ref[0]
    x1 = x_ref[0] + mod[2:3] * a
    x1_ref[0] = x1
    h2 = _rms(x1, gf_ref[...]) * (1.0 + mod[4:5]) + mod[3:4]
    bits = lax.bitcast_convert_type(h2.astype(BF16).astype(F32), U32)
    half = h2.shape[1] // 2
    hp_ref[0] = (bits[:, :half] >> 16) | (bits[:, half:] & jnp.uint32(0xFFFF0000))

    logits = lax.dot_general(wr_ref[...], h2, (((1,), (1,)), ((), ())), precision=lax.Precision.HIGHEST,
                             preferred_element_type=F32) + br_ref[...]
    e_iota = lax.broadcasted_iota(I32, logits.shape, 0).astype(F32)
    vals, idxs = [], []
    for _ in range(TOP_K):
        m = jnp.max(logits, axis=0, keepdims=True)
        i = jnp.min(jnp.where(logits == m, e_iota, float(N_EXPERTS)), axis=0, keepdims=True)
        vals.append(m)
        idxs.append(i)
        logits = jnp.where(e_iota == i, -jnp.inf, logits)
    ex = [jnp.exp(v - vals[0]) for v in vals]
    tot = ex[0] + ex[1] + ex[2] + ex[3]
    idx_ref[0] = jnp.concatenate(idxs, axis=0).astype(I32)
    gate_ref[0] = jnp.concatenate([e / tot for e in ex], axis=0)


def _out(x, om, od, w_out_p, mod, g_ffn, w_r, b_r, t):
    b, s, d = x.shape
    ns = s // t
    full = lambda shape: pl.BlockSpec(shape, lambda i, j: (0,) * len(shape))
    tok = lambda w: pl.BlockSpec((1, t, w), lambda i, j: (i, j, 0))
    flat = pl.BlockSpec((1, TOP_K, t), lambda i, j: (i * ns + j, 0, 0))
    x1, hp, idx, gates = pl.pallas_call(
        _out_kernel,
        grid=(b, ns),
        in_specs=[tok(d), tok(MLA_PAD), tok(DIFF_QKV), full((MLA_PAD + DIFF_QKV, d)),
                  pl.BlockSpec((1, 6, d), lambda i, j: (i, 0, 0)), full((1, d)),
                  full((N_EXPERTS, d)), full((N_EXPERTS, 1))],
        out_specs=[tok(d), tok(d // 2), flat, flat],
        out_shape=[jax.ShapeDtypeStruct((b, s, d), F32), jax.ShapeDtypeStruct((b, s, d // 2), U32),
                   jax.ShapeDtypeStruct((b * ns, TOP_K, t), I32), jax.ShapeDtypeStruct((b * ns, TOP_K, t), F32)],
        compiler_params=_cparams(("arbitrary", "arbitrary")),
    )(x, om, od, w_out_p, mod, g_ffn, w_r, b_r)
    unflat = lambda a: a.transpose(1, 0, 2).reshape(TOP_K, b * s)
    return x1, hp, unflat(idx), unflat(gates)


def _rank_kernel(idx_ref, rank_ref, cnt_ref, carry_ref):
    step = pl.program_id(0)

    @pl.when(step == 0)
    def _():
        carry_ref[...] = jnp.zeros(carry_ref.shape, F32)

    idx = idx_ref[...]
    t = idx.shape[1]
    e_iota = lax.broadcasted_iota(I32, (N_EXPERTS, t), 0)
    hits = [e_iota == idx[k:k + 1, :] for k in range(TOP_K)]
    onehot = sum(h.astype(F32) for h in hits)
    earlier = (lax.broadcasted_iota(I32, (t, t), 0) < lax.broadcasted_iota(I32, (t, t), 1)).astype(BF16)
    before = jnp.dot(onehot.astype(BF16), earlier, preferred_element_type=F32) + carry_ref[...]
    rank_ref[...] = jnp.concatenate(
        [jnp.sum(jnp.where(h, before, 0.0), axis=0, keepdims=True) for h in hits], axis=0).astype(I32)
    carry_ref[...] = carry_ref[...] + jnp.sum(onehot, axis=1, keepdims=True)
    cnt_ref[...] = carry_ref[...].astype(I32)


def _rank(idx, t):
    n = idx.shape[1]
    return pl.pallas_call(
        _rank_kernel,
        grid=(n // t,),
        in_specs=[pl.BlockSpec((TOP_K, t), lambda i: (0, i))],
        out_specs=[pl.BlockSpec((TOP_K, t), lambda i: (0, i)), pl.BlockSpec((N_EXPERTS, 1), lambda i: (0, 0))],
        out_shape=[jax.ShapeDtypeStruct((TOP_K, n), I32), jax.ShapeDtypeStruct((N_EXPERTS, 1), I32)],
        scratch_shapes=[pltpu.VMEM((N_EXPERTS, 1), F32)],
        compiler_params=_cparams(("arbitrary",)),
    )(idx)


def _dispatch_kernel(dest_ref, dnxt_ref, hp_ref, xs_in_ref, xs_ref, dsm, sem, idx_sem, *, t, n_steps):
    del xs_in_ref
    i = pl.program_id(0)
    slot = i % 2

    @pl.when(i == 0)
    def _():
        cp = pltpu.make_async_copy(dest_ref.at[0], dsm.at[0], idx_sem)
        cp.start()
        cp.wait()

    nxt = pltpu.make_async_copy(dnxt_ref.at[0], dsm.at[1 - slot], idx_sem)

    @pl.when(i + 1 < n_steps)
    def _():
        nxt.start()

    def issue(tok, c):
        for k in range(TOP_K):
            pltpu.make_async_copy(hp_ref.at[pl.ds(tok, 1)], xs_ref.at[pl.ds(dsm[slot, k, tok], 1)], sem).start()
        return c

    lax.fori_loop(0, t, issue, 0, unroll=ISSUE_UNROLL)
    for k in range(TOP_K):
        pltpu.make_async_copy(hp_ref, xs_ref.at[pl.ds(0, t)], sem).wait()

    @pl.when(i + 1 < n_steps)
    def _():
        nxt.wait()


def _dispatch(dest3, hp, n_slots, t):
    n, w = hp.shape
    n_steps = n // t
    zeros = jnp.zeros((n_slots, w), U32)
    return pl.pallas_call(
        functools.partial(_dispatch_kernel, t=t, n_steps=n_steps),
        grid=(n_steps,),
        in_specs=[pl.BlockSpec((1, TOP_K, t), lambda i: (i, 0, 0)),
                  pl.BlockSpec((1, TOP_K, t), lambda i: (jnp.minimum(i + 1, n_steps - 1), 0, 0)),
                  pl.BlockSpec((t, w), lambda i: (i, 0)),
                  pl.BlockSpec(memory_space=pl.ANY)],
        out_specs=pl.BlockSpec(memory_space=pl.ANY),
        out_shape=jax.ShapeDtypeStruct((n_slots, w), U32),
        scratch_shapes=[pltpu.SMEM((2, TOP_K, t), I32), pltpu.SemaphoreType.DMA, pltpu.SemaphoreType.DMA],
        input_output_aliases={3: 0},
        compiler_params=_cparams(("arbitrary",)),
    )(dest3, dest3, hp, zeros)


def _expert_kernel(te_ref, nu_ref, xs_ref, wgu_ref, bgu_ref, wd_ref, bd_ref, ys_ref):
    del te_ref

    @pl.when(pl.program_id(0) < nu_ref[0])
    def _():
        p = xs_ref[...]
        x_lo = lax.bitcast_convert_type(p << 16, F32).astype(BF16)
        x_hi = lax.bitcast_convert_type(p & jnp.uint32(0xFFFF0000), F32).astype(BF16)
        half = p.shape[1]
        gu = (jnp.dot(x_lo, wgu_ref[0, 0:half, :], preferred_element_type=F32)
              + jnp.dot(x_hi, wgu_ref[0, half:, :], preferred_element_type=F32) + bgu_ref[0])
        f = gu.shape[1] // 2
        glu = jnp.minimum(gu[:, :f], SWIGLU_LIMIT)
        lin = jnp.clip(gu[:, f:], -SWIGLU_LIMIT, SWIGLU_LIMIT)
        act = glu * jax.nn.sigmoid(SWIGLU_ALPHA * glu) * (lin + 1.0)
        ys_ref[...] = jnp.dot(act.astype(BF16), wd_ref[0], preferred_element_type=F32) + bd_ref[0]


def _experts(tile_e, n_used, xs, w_gu, b_gu, w_down, b_down, tm):
    n_slots, half = xs.shape
    e, d, f2 = w_gu.shape
    n_tiles = n_slots // tm
    row = lambda i, te, nu: (jnp.minimum(i, nu[0] - 1), 0)
    wsel = lambda i, te, nu: (te[i], 0, 0)
    grid_spec = pltpu.PrefetchScalarGridSpec(
        num_scalar_prefetch=2,
        grid=(n_tiles,),
        in_specs=[pl.BlockSpec((tm, half), row),
                  pl.BlockSpec((1, d, f2), wsel), pl.BlockSpec((1, 1, f2), wsel),
                  pl.BlockSpec((1, f2 // 2, d), wsel), pl.BlockSpec((1, 1, d), wsel)],
        out_specs=pl.BlockSpec((tm, d), row),
    )
    return pl.pallas_call(
        _expert_kernel,
        grid_spec=grid_spec,
        out_shape=jax.ShapeDtypeStruct((n_slots, d), F32),
        compiler_params=_cparams(("arbitrary",)),
    )(tile_e, n_used, xs, w_gu, b_gu.reshape(e, 1, f2), w_down, b_down.reshape(e, 1, d))


def _combine_kernel(dcur_ref, dnxt_ref, dnn_ref, gate_ref, x1_ref, mod_ref, gfin_ref, ys_ref,
                    x2_ref, buf, dsm, sem, idx_sem, *, t, n_steps, final):
    i = pl.program_id(0)
    slot = i % 2

    def load_idx(src, s):
        cp = pltpu.make_async_copy(src.at[0], dsm.at[s], idx_sem)
        cp.start()
        cp.wait()

    def issue(s, si):
        def body(tok, c):
            for k in range(TOP_K):
                pltpu.make_async_copy(ys_ref.at[pl.ds(dsm[si, k, tok], 1)], buf.at[s, k, pl.ds(tok, 1)],
                                      sem.at[s]).start()
            return c
        lax.fori_loop(0, t, body, 0, unroll=ISSUE_UNROLL)

    @pl.when(i == 0)
    def _():
        load_idx(dcur_ref, 0)
        issue(0, 0)
        if n_steps > 1:
            load_idx(dnxt_ref, 1)

    ahead = pltpu.make_async_copy(dnn_ref.at[0], dsm.at[(i + 2) % 3], idx_sem)

    @pl.when(i + 2 < n_steps)
    def _():
        ahead.start()

    @pl.when(i + 1 < n_steps)
    def _():
        issue(1 - slot, (i + 1) % 3)

    for k in range(TOP_K):
        pltpu.make_async_copy(ys_ref.at[pl.ds(0, t)], buf.at[slot, k], sem.at[slot]).wait()

    g = gate_ref[...]
    y = g[:, 0:1] * buf[slot, 0]
    for k in range(1, TOP_K):
        y = y + g[:, k:k + 1] * buf[slot, k]
    x2 = x1_ref[...] + mod_ref[0][5:6] * y
    x2_ref[...] = _rms(x2, gfin_ref[...]) if final else x2

    @pl.when(i + 2 < n_steps)
    def _():
        ahead.wait()


def _combine(dest3, gates_t, x1, mod, g_final, ys, s, t, final):
    n, d = x1.shape
    n_steps = n // t
    per_b = s // t
    return pl.pallas_call(
        functools.partial(_combine_kernel, t=t, n_steps=n_steps, final=final),
        grid=(n_steps,),
        in_specs=[pl.BlockSpec((1, TOP_K, t), lambda i: (i, 0, 0)),
                  pl.BlockSpec((1, TOP_K, t), lambda i: (jnp.minimum(i + 1, n_steps - 1), 0, 0)),
                  pl.BlockSpec((1, TOP_K, t), lambda i: (jnp.minimum(i + 2, n_steps - 1), 0, 0)),
                  pl.BlockSpec((t, TOP_K), lambda i: (i, 0)),
                  pl.BlockSpec((t, d), lambda i: (i, 0)),
                  pl.BlockSpec((1, 6, d), lambda i: (i // per_b, 0, 0)),
                  pl.BlockSpec((1, d), lambda i: (0, 0)),
                  pl.BlockSpec(memory_space=pl.ANY)],
        out_specs=pl.BlockSpec((t, d), lambda i: (i, 0)),
        out_shape=jax.ShapeDtypeStruct((n, d), F32),
        scratch_shapes=[pltpu.VMEM((2, TOP_K, t, d), F32), pltpu.SMEM((3, TOP_K, t), I32),
                        pltpu.SemaphoreType.DMA((2,)), pltpu.SemaphoreType.DMA],
        compiler_params=_cparams(("arbitrary",)),
    )(dest3, dest3, dest3, gates_t, x1, mod, g_final, ys)


def _rope_tables(pos, d, lane_pos):
    inv = ROPE_THETA ** (-jnp.arange(0, d, 2, dtype=F32) / d)
    ang = pos.astype(F32)[:, None] * inv[None, :]
    cos, sin = jnp.cos(ang), jnp.sin(ang)
    cos_t = jnp.ones((pos.shape[0], LANES), F32)
    sin_t = jnp.zeros((pos.shape[0], LANES), F32)
    for o in lane_pos:
        cos_t = cos_t.at[:, o:o + d].set(jnp.concatenate([cos, cos], axis=1))
        sin_t = sin_t.at[:, o:o + d].set(jnp.concatenate([-sin, sin], axis=1))
    return cos_t, sin_t


def _prep_weights(w_in, w_uq, w_uk, w_uv, w_out):
    depth, d, _ = w_in.shape
    cuts = [MLA_Q_LORA, MLA_KV_LORA, MLA_ROPE, DIFF_QKV, DIFF_QKV, DIFF_QKV]
    offs = [0]
    for c in cuts:
        offs.append(offs[-1] + c)
    parts = [w_in[:, :, offs[i]:offs[i + 1]] for i in range(6)]
    kr_pad = jnp.pad(parts[2], ((0, 0), (0, 0), (MLA_NOPE, LANES - MLA_NOPE - MLA_ROPE)))
    w_in_p = jnp.concatenate([parts[0], parts[1], kr_pad, parts[3], parts[4], parts[5]], axis=2).astype(BF16)
    qd = MLA_NOPE + MLA_ROPE
    w_uq_p = jnp.pad(w_uq.reshape(depth, MLA_Q_LORA, MLA_HEADS, qd), ((0, 0), (0, 0), (0, 0), (0, LANES - qd)))
    w_uq_p = w_uq_p.reshape(depth, MLA_Q_LORA, MLA_PAD).astype(BF16)
    pad_head = lambda w, n: jnp.pad(w, ((0, 0), (0, 0), (0, 0), (0, LANES - n))).reshape(depth, MLA_KV_LORA, MLA_PAD)
    w_ukv_p = jnp.concatenate([pad_head(w_uk, MLA_NOPE), pad_head(w_uv, MLA_V)], axis=2).astype(BF16)
    wo_m = jnp.pad(w_out[:, :MLA_HEADS * MLA_V].reshape(depth, MLA_HEADS, MLA_V, d),
                   ((0, 0), (0, 0), (0, LANES - MLA_V), (0, 0))).reshape(depth, MLA_PAD, d)
    w_out_p = jnp.concatenate([wo_m, w_out[:, MLA_HEADS * MLA_V:]], axis=1).astype(BF16)
    place = jnp.zeros((MLA_ROPE, 2 * MLA_PAD), F32)
    eye = jnp.eye(MLA_ROPE, dtype=F32)
    for hd in range(MLA_HEADS):
        place = place.at[:, hd * LANES + MLA_NOPE:hd * LANES + MLA_NOPE + MLA_ROPE].set(eye)
    w_uvt_p = jnp.swapaxes(w_ukv_p[:, :, MLA_PAD:], 1, 2)
    return w_in_p, w_uq_p, w_ukv_p, w_out_p, place.astype(BF16), w_uvt_p


def _pick(n, pref):
    t = min(n, pref)
    while n % t:
        t //= 2
    return t


def _moe(hp, idx, gates, x1, mod_l, g_final, wts, s, final):
    w_gu, b_gu, w_down, b_down = wts
    n = hp.shape[0]
    rank, counts = _rank(idx, _pick(n, RANK_TILE))
    counts = counts[:, 0]
    tm = MOE_TILE if n * TOP_K >= N_EXPERTS * MOE_TILE else MOE_TILE_SMALL
    padded = (counts + tm - 1) // tm * tm
    pad_ends = jnp.cumsum(padded)
    pad_starts = pad_ends - padded
    dest = rank + jnp.sum(jnp.where(idx[None] == jnp.arange(N_EXPERTS, dtype=I32)[:, None, None],
                                    pad_starts[:, None, None], 0), axis=0)
    n_tiles = -(-(n * TOP_K) // tm) + N_EXPERTS
    tile_e = jnp.minimum(jnp.sum(pad_ends[None, :] <= (jnp.arange(n_tiles, dtype=I32) * tm)[:, None], axis=1),
                         N_EXPERTS - 1).astype(I32)
    n_used = (pad_ends[-1:] // tm).astype(I32)
    td = _pick(n, DISPATCH_TILE)
    xs = _dispatch(dest.reshape(TOP_K, n // td, td).transpose(1, 0, 2), hp, n_tiles * tm, td)
    ys = _experts(tile_e, n_used, xs, w_gu, b_gu, w_down, b_down, tm)
    tc = _pick(s, COMBINE_TILE)
    return _combine(dest.reshape(TOP_K, n // tc, tc).transpose(1, 0, 2), gates.T, x1, mod_l, g_final, ys, s, tc,
                    final)


def _trunk(x, c, caches, prm, prepped, q_off):
    b, s, d = x.shape
    depth = prm['w_in'].shape[0]
    w_in_p, w_uq_p, w_ukv_p, w_out_p, place, w_uvt_p, w_ada_b, w_gu_b, w_down_b = prepped
    rows = -(-b // 16) * 16
    mod = _ada(jnp.pad(c, ((0, rows - b), (0, 0))), w_ada_b, prm['b_ada'])[:, :b].reshape(depth, b, 6, d)
    pos = q_off + jnp.arange(s, dtype=I32)
    cos_q, sin_q = _rope_tables(pos, MLA_ROPE, (MLA_NOPE,))
    cos_d, sin_d = _rope_tables(pos, DIFF_HD, (0, DIFF_HD))
    t = _pick(s, TOK_TILE)
    stacks = tuple(jnp.zeros((depth, b, s, w), F32) for w in ROW_WIDTHS)
    row = lambda v: v.reshape(1, -1)
    key_major = caches is None and q_off == 0 and t == ATTN_TK
    for l in range(depth):
        kv_w = (w_ukv_p[l][:, :MLA_PAD], w_uvt_p[l]) if key_major else None
        qm, qd, stacks, kv = _proj(x, mod[l], row(prm['g_attn'][l]), w_in_p[l], row(prm['g_q'][l]), w_uq_p[l],
                                   row(prm['g_kv'][l]), (cos_q, sin_q, cos_d, sin_d), t, l, depth, stacks, kv_w)
        lam_init = 0.8 - 0.6 * math.exp(-0.3 * l)
        diff_args = (row(prm['lam_q1'][l]), row(prm['lam_k1'][l]), row(prm['lam_q2'][l]),
                     row(prm['lam_k2'][l]), row(prm['g_diff'][l]))
        if key_major:
            km, vm_t, dkb, dv_t = kv
            om = _attention_t(qm, km, vm_t, MLA_HEADS, _pick(s, ATTN_TQ))
            od = _attention_t(qd, dkb, dv_t, DIFF_HEADS, _pick(s, ATTN_TQ // 2), diff_args, lam_init)
        else:
            if caches is not None:
                lat_a, kr_a, dk_a, dv_a = (jnp.concatenate([cc[l].reshape(b, cc.shape[2], -1), st[l]], axis=1)
                                           for cc, st in zip(caches, stacks))
            else:
                lat_a, kr_a, dk_a, dv_a = (st[l] for st in stacks)
            sk = lat_a.shape[1]
            tk = ATTN_TK if sk % ATTN_TK == 0 else sk
            tkv = max(t_ for t_ in range(16, min(sk, 2 * TOK_TILE) + 1, 16) if sk % t_ == 0)
            km, vm, dkb, dvb = _kv(lat_a, kr_a, dk_a, dv_a, w_ukv_p[l], place, tkv)
            om = _attention(qm, km, vm, MLA_HEADS, _pick(s, ATTN_TQ), tk, q_off)
            od = _attention(qd, dkb, dvb, DIFF_HEADS, _pick(s, ATTN_TQ // 2), tk, q_off, diff_args, lam_init)
        x1, hp, idx, gates = _out(x, om, od, w_out_p[l], mod[l], row(prm['g_ffn'][l]),
                                  prm['w_router'][l].T, prm['b_router'][l].reshape(-1, 1), t)
        x = _moe(hp.reshape(b * s, d // 2), idx, gates, x1.reshape(b * s, d), mod[l], row(prm['g_final']),
                 (w_gu_b[l], prm['b_gate_up'][l], w_down_b[l], prm['b_down'][l]), s, l == depth - 1).reshape(b, s, d)
    lat_n, kr_n, dk_n, dv_n = stacks
    shp = (depth, b, s, DIFF_HEADS, 2 * DIFF_HD)
    return x, (lat_n, kr_n, dk_n.reshape(shp), dv_n.reshape(shp))


def kernel(x_prompt, x_sample, c_prompt, c_sample, cache_mla_latent, cache_mla_krope, cache_diff_k, cache_diff_v,
           w_ada, b_ada, g_attn, g_ffn, w_in, g_q, w_uq, g_kv, w_uk, w_uv, lam_q1, lam_k1, lam_q2, lam_k2, g_diff,
           w_out, w_router, b_router, w_gate_up, b_gate_up, w_down, b_down, g_final):
    prm = {'b_ada': b_ada, 'g_attn': g_attn, 'g_ffn': g_ffn, 'w_in': w_in, 'g_q': g_q, 'g_kv': g_kv,
           'lam_q1': lam_q1, 'lam_k1': lam_k1, 'lam_q2': lam_q2, 'lam_k2': lam_k2, 'g_diff': g_diff,
           'w_router': w_router, 'b_router': b_router, 'b_gate_up': b_gate_up, 'b_down': b_down,
           'g_final': g_final}
    prepped = _prep_weights(w_in, w_uq, w_uk, w_uv, w_out) + (
        w_ada.astype(BF16), w_gate_up.astype(BF16), w_down.astype(BF16))
    y_p, rows_p = _trunk(x_prompt, c_prompt, None, prm, prepped, 0)
    caches = (cache_mla_latent, cache_mla_krope, cache_diff_k, cache_diff_v)
    y_s, rows_s = _trunk(x_sample, c_sample, caches, prm, prepped, cache_mla_latent.shape[2])
    return (y_p, y_s) + rows_p + rows_s
```
